```python
import math
import jax, jax.numpy as jnp
from jax import lax
import numpy as np

D_MODEL = 1024
BATCH = 16
SEQ = 4096
DEPTH = 4
DEC_BATCH = 16
DEC_SEQ = 2048
PAST_LEN = 128

D_PLE = 256
LRU_WIDTH = D_MODEL // 2
LRU_BLOCKS = 8
LRU_BLOCK_W = LRU_WIDTH // LRU_BLOCKS
CONV_W = 4
CONV_LEFT = 2
LRU_C = 8.0
MLA_HEADS = 8
QK_NOPE = 64
QK_ROPE = 32
V_DIM = 64
Q_RANK = 256
KV_RANK = 128
ROPE_THETA = 10000.0
Q_BLOCK = 128
MLA_WIDTH = MLA_HEADS * V_DIM
D_MIX = LRU_WIDTH + MLA_WIDTH
D_IN = 2 * LRU_WIDTH + Q_RANK + KV_RANK + QK_ROPE
D_FF = 2816
N_EXPERTS = 8
TOP_K = 2
N_DENSE = (DEPTH + 1) // 2
N_MOE = DEPTH // 2
EPS = 1e-6

kernel_name = "hybrid_rglru_mla_encoder"


def rmsnorm(x, g):
    xf = x.astype(jnp.float32)
    y = xf * lax.rsqrt(jnp.mean(xf * xf, axis=-1, keepdims=True) + EPS)
    return (y * g.astype(jnp.float32)).astype(x.dtype)


def rope_tables(S):
    pos = jnp.arange(S, dtype=jnp.float32)
    inv = ROPE_THETA ** (-jnp.arange(0, QK_ROPE, 2, dtype=jnp.float32) / QK_ROPE)
    ang = pos[:, None] * inv[None, :]
    return jnp.cos(ang), jnp.sin(ang)


def apply_rope(x, cos, sin):
    half = QK_ROPE // 2
    xf = x.astype(jnp.float32)
    x1, x2 = xf[..., :half], xf[..., half:]
    return jnp.concatenate([x1 * cos - x2 * sin, x2 * cos + x1 * sin], axis=-1).astype(x.dtype)


def centred_depthwise_conv(x, w, b):
    S = x.shape[1]
    xp = jnp.pad(x, ((0, 0), (CONV_LEFT, CONV_W - 1 - CONV_LEFT), (0, 0)))
    y = b
    for k in range(CONV_W):
        y = y + xp[:, k:k + S, :] * w[k]
    return y


def _lin_combine(e1, e2):
    a1, b1 = e1
    a2, b2 = e2
    return a1 * a2, a2 * b1 + b2


def rglru_direction(x, w_a, b_a, w_x, b_x, lam, reverse):
    B, S, W = x.shape
    xb = x.reshape(B, S, LRU_BLOCKS, LRU_BLOCK_W)
    r = jax.nn.sigmoid((jnp.einsum('bsnc,ncd->bsnd', xb, w_a).reshape(B, S, W) + b_a).astype(jnp.float32))
    i = jax.nn.sigmoid((jnp.einsum('bsnc,ncd->bsnd', xb, w_x).reshape(B, S, W) + b_x).astype(jnp.float32))
    log_a = -LRU_C * jax.nn.softplus(-lam.astype(jnp.float32)) * r
    a = jnp.exp(log_a)
    mult = jnp.sqrt(-jnp.expm1(2.0 * log_a))
    u = mult * (i * x.astype(jnp.float32))
    _, h = lax.associative_scan(_lin_combine, (a, u), axis=1, reverse=reverse)
    return h


def mla_attention(c_q, c_kv, k_pe, q_norm, w_uq, kv_norm, w_ukv, cos, sin):
    B, S, _ = c_q.shape
    q = jnp.einsum('bsr,rhd->bshd', rmsnorm(c_q, q_norm), w_uq)
    q_nope, q_pe = q[..., :QK_NOPE], q[..., QK_NOPE:]
    q_pe = apply_rope(q_pe, cos[:, None, :], sin[:, None, :])
    k_pe = apply_rope(k_pe, cos, sin)
    kv = jnp.einsum('bsr,rhd->bshd', rmsnorm(c_kv, kv_norm), w_ukv)
    k_nope, v = kv[..., :QK_NOPE], kv[..., QK_NOPE:]
    scale = 1.0 / math.sqrt(QK_NOPE + QK_ROPE)
    nq = S // Q_BLOCK
    qn_b = q_nope.reshape(B, nq, Q_BLOCK, MLA_HEADS, QK_NOPE).transpose(1, 0, 2, 3, 4)
    qp_b = q_pe.reshape(B, nq, Q_BLOCK, MLA_HEADS, QK_ROPE).transpose(1, 0, 2, 3, 4)

    def block(args):
        qn, qp = args
        s = (jnp.einsum('bqhd,bkhd->bhqk', qn, k_nope) + jnp.einsum('bqhd,bkd->bhqk', qp, k_pe))
        p = jax.nn.softmax(s.astype(jnp.float32) * scale, axis=-1).astype(v.dtype)
        return jnp.einsum('bhqk,bkhd->bqhd', p, v)

    o = lax.map(block, (qn_b, qp_b))
    return o.transpose(1, 0, 2, 3, 4).reshape(B, S, MLA_WIDTH)


def swiglu(x, wg, wu, wd):
    return (jax.nn.silu(x @ wg) * (x @ wu)) @ wd


def moe_swiglu(x, w_router, wg, wu, wd):
    logits = (x @ w_router).astype(jnp.float32)
    top_v, top_i = lax.top_k(logits, TOP_K)
    wts = jax.nn.softmax(top_v, axis=-1)
    gates = jnp.sum(jax.nn.one_hot(top_i, N_EXPERTS, dtype=jnp.float32) * wts[..., None], axis=-2)
    gates = gates.astype(x.dtype)
    out = jnp.zeros_like(x)
    for e in range(N_EXPERTS):
        out = out + gates[..., e:e + 1] * swiglu(x, wg[e], wu[e], wd[e])
    return out


def trunk(x, p, W):
    S = x.shape[1]
    cos, sin = rope_tables(S)
    h = x
    for l in range(DEPTH):
        hn = rmsnorm(h, W['norm_mix'][l])
        proj = hn @ W['w_in'][l]
        o1 = LRU_WIDTH
        o2 = o1 + LRU_WIDTH
        o3 = o2 + Q_RANK
        o4 = o3 + KV_RANK
        x_lru, g_lru = proj[..., :o1], proj[..., o1:o2]
        c_q, c_kv, k_pe = proj[..., o2:o3], proj[..., o3:o4], proj[..., o4:]
        xc = centred_depthwise_conv(x_lru, W['conv_w'][l], W['conv_b'][l])
        h_f = rglru_direction(xc, W['lru_wa'][l, 0], W['lru_ba'][l, 0], W['lru_wx'][l, 0], W['lru_bx'][l, 0],
                              W['lru_lambda'][l, 0], reverse=False)
        h_b = rglru_direction(xc, W['lru_wa'][l, 1], W['lru_ba'][l, 1], W['lru_wx'][l, 1], W['lru_bx'][l, 1],
                              W['lru_lambda'][l, 1], reverse=True)
        y_lru = (h_f + h_b).astype(x.dtype) * jax.nn.gelu(g_lru)
        y_mla = mla_attention(c_q, c_kv, k_pe, W['q_norm'][l], W['w_uq'][l], W['kv_norm'][l], W['w_ukv'][l], cos, sin)
        h = h + jnp.concatenate([y_lru, y_mla], axis=-1) @ W['w_out'][l]
        hn = rmsnorm(h, W['norm_ffn'][l])
        j = l // 2
        if l % 2 == 0:
            ff = swiglu(hn, W['ff_wg'][j], W['ff_wu'][j], W['ff_wd'][j])
        else:
            ff = moe_swiglu(hn, W['moe_router'][j], W['moe_wg'][j], W['moe_wu'][j], W['moe_wd'][j])
        h = h + ff
        gate = jax.nn.sigmoid(rmsnorm(h, W['norm_ple'][l]) @ W['ple_gate'][l])
        h = h + (p[l] @ W['ple_proj'][l]) * gate
    return rmsnorm(h, W['final_norm'])


def setup_inputs(seed: int = 0) -> dict:
    key = jax.random.key(seed)
    ks = jax.random.split(key, 40)
    f32 = jnp.float32

    def nrm(k, shape, scale):
        return jax.random.normal(k, shape, f32) * scale

    def gain(k, shape):
        return 1.0 + 0.01 * jax.random.normal(k, shape, f32)

    a0 = jax.random.uniform(ks[10], (DEPTH, 2, LRU_WIDTH), f32, 0.9, 0.999)
    base = a0 ** (1.0 / LRU_C)
    lru_lambda = jnp.log(base) - jnp.log1p(-base)
    return {
        "x_prompt": nrm(ks[0], (BATCH, SEQ, D_MODEL), 1.0),
        "x_sample": nrm(ks[1], (DEC_BATCH, DEC_SEQ, D_MODEL), 1.0),
        "p_prompt": nrm(ks[2], (DEPTH, BATCH, SEQ, D_PLE), 1.0),
        "p_sample": nrm(ks[3], (DEPTH, DEC_BATCH, DEC_SEQ, D_PLE), 1.0),
        "norm_mix": gain(ks[4], (DEPTH, D_MODEL)),
        "w_in": nrm(ks[5], (DEPTH, D_MODEL, D_IN), D_MODEL ** -0.5),
        "conv_w": nrm(ks[6], (DEPTH, CONV_W, LRU_WIDTH), CONV_W ** -0.5),
        "conv_b": nrm(ks[7], (DEPTH, LRU_WIDTH), 0.01),
        "lru_wa": nrm(ks[8], (DEPTH, 2, LRU_BLOCKS, LRU_BLOCK_W, LRU_BLOCK_W), LRU_BLOCK_W ** -0.5),
        "lru_ba": nrm(ks[9], (DEPTH, 2, LRU_WIDTH), 0.01),
        "lru_wx": nrm(ks[11], (DEPTH, 2, LRU_BLOCKS, LRU_BLOCK_W, LRU_BLOCK_W), LRU_BLOCK_W ** -0.5),
        "lru_bx": nrm(ks[12], (DEPTH, 2, LRU_WIDTH), 0.01),
        "lru_lambda": lru_lambda,
        "q_norm": gain(ks[13], (DEPTH, Q_RANK)),
        "w_uq": nrm(ks[14], (DEPTH, Q_RANK, MLA_HEADS, QK_NOPE + QK_ROPE), Q_RANK ** -0.5),
        "kv_norm": gain(ks[15], (DEPTH, KV_RANK)),
        "w_ukv": nrm(ks[16], (DEPTH, KV_RANK, MLA_HEADS, QK_NOPE + V_DIM), KV_RANK ** -0.5),
        "w_out": nrm(ks[17], (DEPTH, D_MIX, D_MODEL), D_MIX ** -0.5),
        "norm_ffn": gain(ks[18], (DEPTH, D_MODEL)),
        "ff_wg": nrm(ks[19], (N_DENSE, D_MODEL, D_FF), D_MODEL ** -0.5),
        "ff_wu": nrm(ks[20], (N_DENSE, D_MODEL, D_FF), D_MODEL ** -0.5),
        "ff_wd": nrm(ks[21], (N_DENSE, D_FF, D_MODEL), D_FF ** -0.5),
        "moe_router": nrm(ks[22], (N_MOE, D_MODEL, N_EXPERTS), D_MODEL ** -0.5),
        "moe_wg": nrm(ks[23], (N_MOE, N_EXPERTS, D_MODEL, D_FF), D_MODEL ** -0.5),
        "moe_wu": nrm(ks[24], (N_MOE, N_EXPERTS, D_MODEL, D_FF), D_MODEL ** -0.5),
        "moe_wd": nrm(ks[25], (N_MOE, N_EXPERTS, D_FF, D_MODEL), D_FF ** -0.5),
        "norm_ple": gain(ks[26], (DEPTH, D_MODEL)),
        "ple_gate": nrm(ks[27], (DEPTH, D_MODEL, D_MODEL), D_MODEL ** -0.5),
        "ple_proj": nrm(ks[28], (DEPTH, D_PLE, D_MODEL), D_PLE ** -0.5),
        "final_norm": gain(ks[29], (D_MODEL,)),
    }


def reference(x_prompt, x_sample, p_prompt, p_sample, norm_mix, w_in, conv_w, conv_b, lru_wa, lru_ba, lru_wx,
              lru_bx, lru_lambda, q_norm, w_uq, kv_norm, w_ukv, w_out, norm_ffn, ff_wg, ff_wu, ff_wd,
              moe_router, moe_wg, moe_wu, moe_wd, norm_ple, ple_gate, ple_proj, final_norm):
    W = dict(norm_mix=norm_mix, w_in=w_in, conv_w=conv_w, conv_b=conv_b, lru_wa=lru_wa, lru_ba=lru_ba,
             lru_wx=lru_wx, lru_bx=lru_bx, lru_lambda=lru_lambda, q_norm=q_norm, w_uq=w_uq, kv_norm=kv_norm,
             w_ukv=w_ukv, w_out=w_out, norm_ffn=norm_ffn, ff_wg=ff_wg, ff_wu=ff_wu, ff_wd=ff_wd,
             moe_router=moe_router, moe_wg=moe_wg, moe_wu=moe_wu, moe_wd=moe_wd, norm_ple=norm_ple,
             ple_gate=ple_gate, ple_proj=ple_proj, final_norm=final_norm)
    y_prompt = trunk(x_prompt, p_prompt, W)
    y_sample = trunk(x_sample, p_sample, W)
    return (y_prompt, y_sample)
```

```python
import functools
import math

import jax
import jax.numpy as jnp
from jax import lax
from jax.experimental import pallas as pl
from jax.experimental.pallas import tpu as pltpu

D_MODEL = 1024
DEPTH = 4
D_PLE = 256
LRU_WIDTH = 512
LRU_BLOCKS = 8
LRU_BLOCK_W = 64
CONV_W = 4
CONV_LEFT = 2
LRU_C = 8.0
MLA_HEADS = 8
QK_NOPE = 64
QK_ROPE = 32
V_DIM = 64
Q_RANK = 256
KV_RANK = 128
ROPE_THETA = 10000.0
D_FF = 2816
N_EXPERTS = 8
EPS = 1e-6

HEAD_PAD = 128
MLA_PAD = MLA_HEADS * HEAD_PAD
D_IN_MAIN = 2 * LRU_WIDTH + Q_RANK + KV_RANK
D_IN_EXT = D_IN_MAIN + 2 * HEAD_PAD
SUBLANES = 8
HALO = SUBLANES
VMEM_LIMIT = 56 * 1024 * 1024

F32 = jnp.float32
BF16 = jnp.bfloat16


def _rms(x, g):
    return x * lax.rsqrt(jnp.mean(x * x, axis=-1, keepdims=True) + EPS) * g


def _gelu_tanh(x):
    c = math.sqrt(2.0 / math.pi)
    return x * (0.5 * (1.0 + jnp.tanh(c * (x + 0.044715 * (x * x * x)))))


def _dot(a, b):
    return jnp.dot(a, b, preferred_element_type=F32)


def _inproj_kernel(h_ref, gmix_ref, win_ref, qn_ref, wqa_ref, wqb_ref, kvn_ref, wk_ref, wv_ref,
                   cos_ref, sin_ref, xl_ref, gg_ref, q_ref, k_ref, v_ref):
    hn = _rms(h_ref[...], gmix_ref[...]).astype(BF16)
    proj = _dot(hn, win_ref[...])
    o1 = LRU_WIDTH
    o2 = 2 * LRU_WIDTH
    o3 = o2 + Q_RANK
    o4 = o3 + KV_RANK
    o5 = o4 + HEAD_PAD
    xl_ref[...] = proj[:, :o1]
    gg_ref[...] = _gelu_tanh(proj[:, o1:o2]).astype(BF16)
    cqn = _rms(proj[:, o2:o3], qn_ref[...]).astype(BF16)
    ckvn = _rms(proj[:, o3:o4], kvn_ref[...]).astype(BF16)
    cos = cos_ref[...]
    sin = sin_ref[...]
    kpe = proj[:, o4:o5] * cos + proj[:, o5:] * sin
    qa = _dot(cqn, wqa_ref[...])
    qb = _dot(cqn, wqb_ref[...])
    kk = _dot(ckvn, wk_ref[...])
    v_ref[...] = _dot(ckvn, wv_ref[...]).astype(BF16)
    scale = 1.0 / math.sqrt(QK_NOPE + QK_ROPE)
    for hd in range(MLA_HEADS):
        sl = slice(hd * HEAD_PAD, (hd + 1) * HEAD_PAD)
        q_ref[:, sl] = ((qa[:, sl] * cos + qb[:, sl] * sin) * scale).astype(BF16)
        k_ref[:, sl] = (kk[:, sl] + kpe).astype(BF16)


def _inproj(h, wl, cos_t, sin_t, S, tm):
    T = h.shape[0]
    nblk_seq = S // tm
    row = lambda i: (i, 0)
    const = lambda i: (0, 0)
    pos = lambda i: (i % nblk_seq, 0)
    return pl.pallas_call(
        _inproj_kernel,
        grid=(T // tm,),
        in_specs=[
            pl.BlockSpec((tm, D_MODEL), row),
            pl.BlockSpec((1, D_MODEL), const),
            pl.BlockSpec((D_MODEL, D_IN_EXT), const),
            pl.BlockSpec((1, Q_RANK), const),
            pl.BlockSpec((Q_RANK, MLA_PAD), const),
            pl.BlockSpec((Q_RANK, MLA_PAD), const),
            pl.BlockSpec((1, KV_RANK), const),
            pl.BlockSpec((KV_RANK, MLA_PAD), const),
            pl.BlockSpec((KV_RANK, MLA_PAD), const),
            pl.BlockSpec((tm, HEAD_PAD), pos),
            pl.BlockSpec((tm, HEAD_PAD), pos),
        ],
        out_specs=[
            pl.BlockSpec((tm, LRU_WIDTH), row),
            pl.BlockSpec((tm, LRU_WIDTH), row),
            pl.BlockSpec((tm, MLA_PAD), row),
            pl.BlockSpec((tm, MLA_PAD), row),
            pl.BlockSpec((tm, MLA_PAD), row),
        ],
        out_shape=[
            jax.ShapeDtypeStruct((T, LRU_WIDTH), F32),
            jax.ShapeDtypeStruct((T, LRU_WIDTH), BF16),
            jax.ShapeDtypeStruct((T, MLA_PAD), BF16),
            jax.ShapeDtypeStruct((T, MLA_PAD), BF16),
            jax.ShapeDtypeStruct((T, MLA_PAD), BF16),
        ],
        compiler_params=pltpu.CompilerParams(dimension_semantics=("arbitrary",),
                                             vmem_limit_bytes=VMEM_LIMIT),
        name="inproj",
    )(h, wl["g_mix"], wl["w_in"], wl["q_norm"], wl["wqa"], wl["wqb"], wl["kv_norm"], wl["wk"], wl["wv"],
      cos_t, sin_t)


def _lru_kernel(x_ref, xp_ref, xn_ref, cw_ref, cb_ref, wa_ref, ba_ref, wx_ref, bx_ref, lam_ref, h_ref,
                xpad, a_s, b_s, carry, *, reverse, tc, nchunks, nblk):
    j = pl.program_id(0)
    blk = (nblk - 1 - j) if reverse else j
    c = blk % nchunks
    first = (c == nchunks - 1) if reverse else (c == 0)

    xpad[0:HALO, :] = jnp.where(c > 0, xp_ref[...], 0.0)
    xpad[HALO:HALO + tc, :] = x_ref[...]
    xpad[HALO + tc:HALO + tc + HALO, :] = jnp.where(c < nchunks - 1, xn_ref[...], 0.0)
    xc = jnp.broadcast_to(cb_ref[...], (tc, LRU_WIDTH))
    for k in range(CONV_W):
        s0 = HALO - CONV_LEFT + k
        xc = xc + xpad[s0:s0 + tc, :] * cw_ref[k:k + 1, :]

    xcb = xc.astype(BF16)
    r = jax.nn.sigmoid(_dot(xcb, wa_ref[...]) + ba_ref[...])
    gi = jax.nn.sigmoid(_dot(xcb, wx_ref[...]) + bx_ref[...])
    z = -lam_ref[...]
    softplus = jnp.maximum(z, 0.0) + jnp.log1p(jnp.exp(-jnp.abs(z)))
    a = jnp.exp((-LRU_C * softplus) * r)
    u = jnp.sqrt(1.0 - a * a) * (gi * xc)

    sub = lax.broadcasted_iota(jnp.int32, (tc, LRU_WIDTH), 0) & (SUBLANES - 1)
    bv = u
    d = 1
    while d < SUBLANES:
        if reverse:
            shift, mask = tc - d, sub < SUBLANES - d
        else:
            shift, mask = d, sub >= d
        a_sh = pltpu.roll(a, shift, 0)
        b_sh = pltpu.roll(bv, shift, 0)
        bv = jnp.where(mask, a * b_sh + bv, bv)
        a = jnp.where(mask, a * a_sh, a)
        d *= 2
    a_s[...] = a
    b_s[...] = bv

    @pl.when(first)
    def _():
        carry[...] = jnp.zeros_like(carry)

    ntiles = tc // SUBLANES

    def body(i, cr):
        t = (ntiles - 1 - i) if reverse else i
        off = pl.multiple_of(t * SUBLANES, SUBLANES)
        ht = a_s[pl.ds(off, SUBLANES), :] * cr + b_s[pl.ds(off, SUBLANES), :]
        h_ref[pl.ds(off, SUBLANES), :] = ht
        edge = ht[0:1, :] if reverse else ht[SUBLANES - 1:SUBLANES, :]
        return jnp.broadcast_to(edge, (SUBLANES, LRU_WIDTH))

    carry[...] = lax.fori_loop(0, ntiles, body, carry[...])


def _lru(x, wl, d, S, tc, reverse):
    T = x.shape[0]
    nchunks = S // tc
    nblk = T // tc
    halo_per_chunk = tc // HALO
    nhalo = T // HALO
    if reverse:
        blk = lambda j: nblk - 1 - j
    else:
        blk = lambda j: j
    main = lambda j: (blk(j), 0)
    prev = lambda j: (jnp.maximum(blk(j) * halo_per_chunk - 1, 0), 0)
    nxt = lambda j: (jnp.minimum((blk(j) + 1) * halo_per_chunk, nhalo - 1), 0)
    const = lambda j: (0, 0)
    vec = pl.BlockSpec((1, LRU_WIDTH), const)
    mat = pl.BlockSpec((LRU_WIDTH, LRU_WIDTH), const)
    return pl.pallas_call(
        functools.partial(_lru_kernel, reverse=reverse, tc=tc, nchunks=nchunks, nblk=nblk),
        grid=(nblk,),
        in_specs=[
            pl.BlockSpec((tc, LRU_WIDTH), main),
            pl.BlockSpec((HALO, LRU_WIDTH), prev),
            pl.BlockSpec((HALO, LRU_WIDTH), nxt),
            pl.BlockSpec((CONV_W, LRU_WIDTH), const),
            vec, mat, vec, mat, vec, vec,
        ],
        out_specs=pl.BlockSpec((tc, LRU_WIDTH), main),
        out_shape=jax.ShapeDtypeStruct((T, LRU_WIDTH), F32),
        scratch_shapes=[
            pltpu.VMEM((tc + 2 * HALO, LRU_WIDTH), F32),
            pltpu.VMEM((tc, LRU_WIDTH), F32),
            pltpu.VMEM((tc, LRU_WIDTH), F32),
            pltpu.VMEM((SUBLANES, LRU_WIDTH), F32),
        ],
        compiler_params=pltpu.CompilerParams(dimension_semantics=("arbitrary",),
                                             vmem_limit_bytes=VMEM_LIMIT),
        name="lru_bwd" if reverse else "lru_fwd",
    )(x, x, x, wl["conv_w"], wl["conv_b"], wl["wa"][d], wl["ba"][d], wl["wx"][d], wl["bx"][d], wl["lam"][d])


def _attn_kernel(q_ref, k_ref, v_ref, o_ref):
    s = lax.dot_general(q_ref[...], k_ref[...], (((1,), (1,)), ((), ())), preferred_element_type=F32)
    m = jnp.max(s, axis=-1, keepdims=True)
    p = jnp.exp(s - m)
    l = jnp.sum(p, axis=-1, keepdims=True)
    o = _dot(p.astype(BF16), v_ref[...])
    o_ref[...] = (o / l).astype(BF16)


def _attention(q, k, v, B, S, tq):
    T = q.shape[0]
    nq = S // tq
    qmap = lambda b, h, i: (b * nq + i, h)
    kvmap = lambda b, h, i: (b, h)
    return pl.pallas_call(
        _attn_kernel,
        grid=(B, MLA_HEADS, nq),
        in_specs=[
            pl.BlockSpec((tq, HEAD_PAD), qmap),
            pl.BlockSpec((S, HEAD_PAD), kvmap),
            pl.BlockSpec((S, HEAD_PAD), kvmap),
        ],
        out_specs=pl.BlockSpec((tq, HEAD_PAD), qmap),
        out_shape=jax.ShapeDtypeStruct((T, MLA_PAD), BF16),
        compiler_params=pltpu.CompilerParams(dimension_semantics=("arbitrary", "arbitrary", "arbitrary"),
                                             vmem_limit_bytes=VMEM_LIMIT),
        name="attention",
    )(q, k, v)


def _outproj_kernel(h_ref, hf_ref, hb_ref, gg_ref, o_ref, wl_ref, wm_ref, gffn_ref, h2_ref, hn2_ref):
    y = ((hf_ref[...] + hb_ref[...]) * gg_ref[...].astype(F32)).astype(BF16)
    h2 = h_ref[...] + _dot(y, wl_ref[...]) + _dot(o_ref[...], wm_ref[...])
    h2_ref[...] = h2
    hn2_ref[...] = _rms(h2, gffn_ref[...]).astype(BF16)


def _outproj(h, hf, hb, gg, o, wl, tm):
    T = h.shape[0]
    row = lambda i: (i, 0)
    const = lambda i: (0, 0)
    return pl.pallas_call(
        _outproj_kernel,
        grid=(T // tm,),
        in_specs=[
            pl.BlockSpec((tm, D_MODEL), row),
            pl.BlockSpec((tm, LRU_WIDTH), row),
            pl.BlockSpec((tm, LRU_WIDTH), row),
            pl.BlockSpec((tm, LRU_WIDTH), row),
            pl.BlockSpec((tm, MLA_PAD), row),
            pl.BlockSpec((LRU_WIDTH, D_MODEL), const),
            pl.BlockSpec((MLA_PAD, D_MODEL), const),
            pl.BlockSpec((1, D_MODEL), const),
        ],
        out_specs=[pl.BlockSpec((tm, D_MODEL), row), pl.BlockSpec((tm, D_MODEL), row)],
        out_shape=[jax.ShapeDtypeStruct((T, D_MODEL), F32), jax.ShapeDtypeStruct((T, D_MODEL), BF16)],
        compiler_params=pltpu.CompilerParams(dimension_semantics=("arbitrary",),
                                             vmem_limit_bytes=VMEM_LIMIT),
        name="outproj",
    )(h, hf, hb, gg, o, wl["w_out_lru"], wl["w_out_mla"], wl["g_ffn"])


FF_CHUNKS = 2
FF_CHUNK = D_FF // FF_CHUNKS
MXU_WIDTH = 256
FF_SUB = [(o, min(MXU_WIDTH, FF_CHUNK - o)) for o in range(0, FF_CHUNK, MXU_WIDTH)]


def _ffn_kernel(hn_ref, h_ref, p_ref, wg_ref, wu_ref, wd_ref, gffn_ref, wr_ref, gple_ref, wpg_ref, wpp_ref,
                gfin_ref, out_ref, acc, gates, *, moe, final, n_experts):
    e = pl.program_id(1)
    c = pl.program_id(2)
    tm = acc.shape[0]

    @pl.when((e == 0) & (c == 0))
    def _():
        acc[...] = jnp.zeros_like(acc)
        if moe:
            x32 = _rms(h_ref[...], gffn_ref[...])
            logits = jnp.dot(x32, wr_ref[...], preferred_element_type=F32, precision=lax.Precision.HIGHEST)
            lane = lax.broadcasted_iota(jnp.int32, logits.shape, 1).astype(F32)
            neg = jnp.float32(-jnp.inf)
            big = jnp.float32(HEAD_PAD)
            logits = jnp.where(lane < N_EXPERTS, logits, neg)
            v1 = jnp.max(logits, axis=-1, keepdims=True)
            i1 = jnp.min(jnp.where(logits == v1, lane, big), axis=-1, keepdims=True)
            rest = jnp.where(lane == i1, neg, logits)
            v2 = jnp.max(rest, axis=-1, keepdims=True)
            i2 = jnp.min(jnp.where(rest == v2, lane, big), axis=-1, keepdims=True)
            e2 = jnp.exp(v2 - v1)
            den = 1.0 + e2
            gates[...] = jnp.where(lane == i1, 1.0 / den, 0.0) + jnp.where(lane == i2, e2 / den, 0.0)

    x = hn_ref[...]
    if moe:
        lane = lax.broadcasted_iota(jnp.int32, (tm, HEAD_PAD), 1)
        ge = jnp.sum(jnp.where(lane == e, gates[...], 0.0), axis=-1, keepdims=True)
    for off, w in FF_SUB:
        g = _dot(x, wg_ref[:, off:off + w])
        u = _dot(x, wu_ref[:, off:off + w])
        mid = (g * jax.nn.sigmoid(g)) * u
        if moe:
            mid = mid * ge
        acc[...] += _dot(mid.astype(BF16), wd_ref[off:off + w, :])

    @pl.when((e == n_experts - 1) & (c == FF_CHUNKS - 1))
    def _():
        h3 = h_ref[...] + acc[...]
        gate = jax.nn.sigmoid(_dot(_rms(h3, gple_ref[...]).astype(BF16), wpg_ref[...]))
        h4 = h3 + _dot(p_ref[...].astype(BF16), wpp_ref[...]) * gate
        if final:
            h4 = _rms(h4, gfin_ref[...])
        out_ref[...] = h4


def _ffn(hn2, h2, p, l, wl, W, tm, final):
    T = h2.shape[0]
    moe = (l % 2 == 1)
    jl = l // 2
    if moe:
        n_experts = N_EXPERTS
        wg, wu, wd = W["moe_wg"], W["moe_wu"], W["moe_wd"]
        wmap = lambda i, e, c: (jl, e, 0, c)
        dmap = lambda i, e, c: (jl, e, c, 0)
        wg_spec = pl.BlockSpec((None, None, D_MODEL, FF_CHUNK), wmap)
        wd_spec = pl.BlockSpec((None, None, FF_CHUNK, D_MODEL), dmap)
    else:
        n_experts = 1
        wg, wu, wd = W["ff_wg"], W["ff_wu"], W["ff_wd"]
        wmap = lambda i, e, c: (jl, 0, c)
        dmap = lambda i, e, c: (jl, c, 0)
        wg_spec = pl.BlockSpec((None, D_MODEL, FF_CHUNK), wmap)
        wd_spec = pl.BlockSpec((None, FF_CHUNK, D_MODEL), dmap)
    row = lambda i, e, c: (i, 0)
    const = lambda i, e, c: (0, 0)
    return pl.pallas_call(
        functools.partial(_ffn_kernel, moe=moe, final=final, n_experts=n_experts),
        grid=(T // tm, n_experts, FF_CHUNKS),
        in_specs=[
            pl.BlockSpec((tm, D_MODEL), row),
            pl.BlockSpec((tm, D_MODEL), row),
            pl.BlockSpec((None, tm, D_PLE), lambda i, e, c: (l, i, 0)),
            wg_spec, wg_spec, wd_spec,
            pl.BlockSpec((1, D_MODEL), const),
            pl.BlockSpec((D_MODEL, HEAD_PAD), const),
            pl.BlockSpec((1, D_MODEL), const),
            pl.BlockSpec((D_MODEL, D_MODEL), const),
            pl.BlockSpec((D_PLE, D_MODEL), const),
            pl.BlockSpec((1, D_MODEL), const),
        ],
        out_specs=pl.BlockSpec((tm, D_MODEL), row),
        out_shape=jax.ShapeDtypeStruct((T, D_MODEL), F32),
        scratch_shapes=[pltpu.VMEM((tm, D_MODEL), F32), pltpu.VMEM((tm, HEAD_PAD), F32)],
        compiler_params=pltpu.CompilerParams(dimension_semantics=("arbitrary", "arbitrary", "arbitrary"),
                                             vmem_limit_bytes=VMEM_LIMIT),
        name="moe" if moe else "ffn",
    )(hn2, h2, p, wg, wu, wd, wl["g_ffn"], wl["w_router"], wl["g_ple"], wl["ple_gate"], wl["ple_proj"],
      W["final_norm"])


def _rot_half(w):
    half = QK_ROPE // 2
    return jnp.concatenate([-w[..., half:], w[..., :half]], axis=-1)


def _prepare_weights(norm_mix, w_in, conv_w, conv_b, lru_wa, lru_ba, lru_wx, lru_bx, lru_lambda, q_norm, w_uq,
                     kv_norm, w_ukv, w_out, norm_ffn, ff_wg, ff_wu, ff_wd, moe_router, moe_wg, moe_wu, moe_wd,
                     norm_ple, ple_gate, ple_proj, final_norm):
    eye = jnp.eye(LRU_BLOCKS, dtype=F32)
    layers = []
    for l in range(DEPTH):
        w_kpe = w_in[l][:, D_IN_MAIN:]
        zl = jnp.zeros((D_MODEL, QK_NOPE), F32)
        zr = jnp.zeros((D_MODEL, HEAD_PAD - QK_NOPE - QK_ROPE), F32)
        w_in_ext = jnp.concatenate([w_in[l][:, :D_IN_MAIN], zl, w_kpe, zr, zl, _rot_half(w_kpe), zr], axis=-1)
        uq = w_uq[l]
        zq = jnp.zeros((Q_RANK, MLA_HEADS, HEAD_PAD - QK_NOPE - QK_ROPE), F32)
        wqa = jnp.concatenate([uq, zq], axis=-1).reshape(Q_RANK, MLA_PAD)
        wqb = jnp.concatenate([jnp.zeros((Q_RANK, MLA_HEADS, QK_NOPE), F32), _rot_half(uq[..., QK_NOPE:]), zq],
                              axis=-1).reshape(Q_RANK, MLA_PAD)
        ukv = w_ukv[l]
        zk = jnp.zeros((KV_RANK, MLA_HEADS, HEAD_PAD - QK_NOPE), F32)
        wk = jnp.concatenate([ukv[..., :QK_NOPE], zk], axis=-1).reshape(KV_RANK, MLA_PAD)
        wv = jnp.concatenate([ukv[..., QK_NOPE:], zk], axis=-1).reshape(KV_RANK, MLA_PAD)
        wo_mla = w_out[l][LRU_WIDTH:].reshape(MLA_HEADS, V_DIM, D_MODEL)
        wo_mla = jnp.pad(wo_mla, ((0, 0), (0, HEAD_PAD - V_DIM), (0, 0))).reshape(MLA_PAD, D_MODEL)

        def blockdiag(w):
            return jnp.einsum("xncd,nm->xncmd", w, eye).reshape(2, LRU_WIDTH, LRU_WIDTH)

        if l % 2 == 1:
            w_router = jnp.pad(moe_router[l // 2], ((0, 0), (0, HEAD_PAD - N_EXPERTS)))
        else:
            w_router = jnp.zeros((D_MODEL, HEAD_PAD), F32)
        layers.append(dict(
            g_mix=norm_mix[l][None], w_in=w_in_ext.astype(BF16),
            q_norm=q_norm[l][None], wqa=wqa.astype(BF16), wqb=wqb.astype(BF16),
            kv_norm=kv_norm[l][None], wk=wk.astype(BF16), wv=wv.astype(BF16),
            conv_w=conv_w[l], conv_b=conv_b[l][None],
            wa=blockdiag(lru_wa[l]).astype(BF16), ba=lru_ba[l][:, None, :],
            wx=blockdiag(lru_wx[l]).astype(BF16), bx=lru_bx[l][:, None, :],
            lam=lru_lambda[l][:, None, :],
            w_out_lru=w_out[l][:LRU_WIDTH].astype(BF16), w_out_mla=wo_mla.astype(BF16),
            g_ffn=norm_ffn[l][None], w_router=w_router,
            g_ple=norm_ple[l][None], ple_gate=ple_gate[l].astype(BF16), ple_proj=ple_proj[l].astype(BF16),
        ))
    W = dict(ff_wg=ff_wg.astype(BF16), ff_wu=ff_wu.astype(BF16), ff_wd=ff_wd.astype(BF16),
             moe_wg=moe_wg.astype(BF16), moe_wu=moe_wu.astype(BF16), moe_wd=moe_wd.astype(BF16),
             final_norm=final_norm[None])
    return layers, W


def _rope_tables(S):
    pos = jnp.arange(S, dtype=F32)
    inv = ROPE_THETA ** (-jnp.arange(0, QK_ROPE, 2, dtype=F32) / QK_ROPE)
    ang = pos[:, None] * inv[None, :]
    cos, sin = jnp.cos(ang), jnp.sin(ang)
    pad = jnp.zeros((S, HEAD_PAD - QK_NOPE - QK_ROPE), F32)
    cos_t = jnp.concatenate([jnp.ones((S, QK_NOPE), F32), cos, cos, pad], axis=-1)
    sin_t = jnp.concatenate([jnp.zeros((S, QK_NOPE), F32), sin, sin, pad], axis=-1)
    return cos_t, sin_t


def _tiles(B, S):
    T = B * S
    return dict(tm=min(512, S), tc=min(1024, S), tq=min(256, S), tf=min(512, T))


def _trunk(x, p, layers, W):
    B, S, _ = x.shape
    T = B * S
    t = _tiles(B, S)
    cos_t, sin_t = _rope_tables(S)
    h = x.reshape(T, D_MODEL)
    p = p.reshape(DEPTH, T, D_PLE)
    for l in range(DEPTH):
        wl = layers[l]
        xl, gg, q, k, v = _inproj(h, wl, cos_t, sin_t, S, t["tm"])
        hf = _lru(xl, wl, 0, S, t["tc"], reverse=False)
        hb = _lru(xl, wl, 1, S, t["tc"], reverse=True)
        o = _attention(q, k, v, B, S, t["tq"])
        h2, hn2 = _outproj(h, hf, hb, gg, o, wl, t["tm"])
        h = _ffn(hn2, h2, p, l, wl, W, t["tf"], final=(l == DEPTH - 1))
    return h.reshape(B, S, D_MODEL)


def kernel(x_prompt, x_sample, p_prompt, p_sample, norm_mix, w_in, conv_w, conv_b, lru_wa, lru_ba, lru_wx, lru_bx, lru_lambda, q_norm, w_uq, kv_norm, w_ukv, w_out, norm_ffn, ff_wg, ff_wu, ff_wd, moe_router, moe_wg, moe_wu, moe_wd, norm_ple, ple_gate, ple_proj, final_norm):
    layers, W = _prepare_weights(norm_mix, w_in, conv_w, conv_b, lru_wa, lru_ba, lru_wx, lru_bx, lru_lambda,
                                 q_norm, w_uq, kv_norm, w_ukv, w_out, norm_ffn, ff_wg, ff_wu, ff_wd, moe_router,
                                 moe_wg, moe_wu, moe_wd, norm_ple, ple_gate, ple_proj, final_norm)
    y_prompt = _trunk(x_prompt, p_prompt, layers, W)
    y_sample = _trunk(x_sample, p_sample, layers, W)
    return (y_prompt, y_sample)
```

```python
import functools
import math

import jax
import jax.numpy as jnp
from jax import lax
from jax.experimental import pallas as pl
from jax.experimental.pallas import tpu as pltpu

D_MODEL = 1024
DEPTH = 4
D_PLE = 256
LRU_WIDTH = 512
LRU_BLOCKS = 8
LRU_BLOCK_W = 64
CONV_W = 4
CONV_LEFT = 2
LRU_C = 8.0
MLA_HEADS = 8
QK_NOPE = 64
QK_ROPE = 32
V_DIM = 64
Q_RANK = 256
KV_RANK = 128
ROPE_THETA = 10000.0
D_FF = 2816
N_EXPERTS = 8
EPS = 1e-6

HEAD_PAD = 128
MLA_PAD = MLA_HEADS * HEAD_PAD
D_IN_MAIN = 2 * LRU_WIDTH + Q_RANK + KV_RANK
D_IN_EXT = D_IN_MAIN + 2 * HEAD_PAD
SUBLANES = 8
HALO = SUBLANES
VMEM_LIMIT = 56 * 1024 * 1024

F32 = jnp.float32
BF16 = jnp.bfloat16


def _rms(x, g):
    return x * lax.rsqrt(jnp.mean(x * x, axis=-1, keepdims=True) + EPS) * g


def _gelu_tanh(x):
    c = math.sqrt(2.0 / math.pi)
    return x * (0.5 * (1.0 + jnp.tanh(c * (x + 0.044715 * (x * x * x)))))


def _dot(a, b):
    return jnp.dot(a, b, preferred_element_type=F32)


def _inproj_kernel(h_ref, gmix_ref, win_ref, qn_ref, wqa_ref, wqb_ref, kvn_ref, wk_ref, wv_ref,
                   cos_ref, sin_ref, xl_ref, gg_ref, q_ref, kt_ref, v_ref):
    hn = _rms(h_ref[...], gmix_ref[...]).astype(BF16)
    proj = _dot(hn, win_ref[...])
    o1 = LRU_WIDTH
    o2 = 2 * LRU_WIDTH
    o3 = o2 + Q_RANK
    o4 = o3 + KV_RANK
    o5 = o4 + HEAD_PAD
    xl_ref[...] = proj[:, :o1]
    gg_ref[...] = _gelu_tanh(proj[:, o1:o2]).astype(BF16)
    cqn = _rms(proj[:, o2:o3], qn_ref[...]).astype(BF16)
    ckvn = _rms(proj[:, o3:o4], kvn_ref[...]).astype(BF16)
    cos = cos_ref[...]
    sin = sin_ref[...]
    kpe = proj[:, o4:o5] * cos + proj[:, o5:] * sin
    qa = _dot(cqn, wqa_ref[...])
    qb = _dot(cqn, wqb_ref[...])
    kk = _dot(ckvn, wk_ref[...])
    vv = _dot(ckvn, wv_ref[...])
    lane = lax.broadcasted_iota(jnp.int32, vv.shape, 1) & (HEAD_PAD - 1)
    v_ref[...] = jnp.where(lane == V_DIM, 1.0, vv).astype(BF16)
    scale = math.log2(math.e) / math.sqrt(QK_NOPE + QK_ROPE)
    for hd in range(MLA_HEADS):
        sl = slice(hd * HEAD_PAD, (hd + 1) * HEAD_PAD)
        q_ref[:, sl] = ((qa[:, sl] * cos + qb[:, sl] * sin) * scale).astype(BF16)
        kt_ref[0, sl, :] = (kk[:, sl] + kpe).T.astype(BF16)


def _inproj(h, wl, cos_t, sin_t, S, tm):
    T = h.shape[0]
    nblk_seq = S // tm
    row = lambda i: (i, 0)
    const = lambda i: (0, 0)
    pos = lambda i: (i % nblk_seq, 0)
    return pl.pallas_call(
        _inproj_kernel,
        grid=(T // tm,),
        in_specs=[
            pl.BlockSpec((tm, D_MODEL), row),
            pl.BlockSpec((1, D_MODEL), const),
            pl.BlockSpec((D_MODEL, D_IN_EXT), const),
            pl.BlockSpec((1, Q_RANK), const),
            pl.BlockSpec((Q_RANK, MLA_PAD), const),
            pl.BlockSpec((Q_RANK, MLA_PAD), const),
            pl.BlockSpec((1, KV_RANK), const),
            pl.BlockSpec((KV_RANK, MLA_PAD), const),
            pl.BlockSpec((KV_RANK, MLA_PAD), const),
            pl.BlockSpec((tm, HEAD_PAD), pos),
            pl.BlockSpec((tm, HEAD_PAD), pos),
        ],
        out_specs=[
            pl.BlockSpec((tm, LRU_WIDTH), row),
            pl.BlockSpec((tm, LRU_WIDTH), row),
            pl.BlockSpec((tm, MLA_PAD), row),
            pl.BlockSpec((1, MLA_PAD, tm), lambda i: (i, 0, 0)),
            pl.BlockSpec((tm, MLA_PAD), row),
        ],
        out_shape=[
            jax.ShapeDtypeStruct((T, LRU_WIDTH), F32),
            jax.ShapeDtypeStruct((T, LRU_WIDTH), BF16),
            jax.ShapeDtypeStruct((T, MLA_PAD), BF16),
            jax.ShapeDtypeStruct((T // tm, MLA_PAD, tm), BF16),
            jax.ShapeDtypeStruct((T, MLA_PAD), BF16),
        ],
        compiler_params=pltpu.CompilerParams(dimension_semantics=("arbitrary",),
                                             vmem_limit_bytes=VMEM_LIMIT),
        name="inproj",
    )(h, wl["g_mix"], wl["w_in"], wl["q_norm"], wl["wqa"], wl["wqb"], wl["kv_norm"], wl["wk"], wl["wv"],
      cos_t, sin_t)


def _lru_kernel(x_ref, xp_ref, xn_ref, cw_ref, cb_ref, wa_ref, ba_ref, wx_ref, bx_ref, lam_ref, h_ref,
                xpad, a_s, b_s, carry, *, reverse, tc, nchunks, nblk):
    j = pl.program_id(0)
    blk = (nblk - 1 - j) if reverse else j
    c = blk % nchunks
    first = (c == nchunks - 1) if reverse else (c == 0)

    xpad[0:HALO, :] = jnp.where(c > 0, xp_ref[...], 0.0)
    xpad[HALO:HALO + tc, :] = x_ref[...]
    xpad[HALO + tc:HALO + tc + HALO, :] = jnp.where(c < nchunks - 1, xn_ref[...], 0.0)
    xc = jnp.broadcast_to(cb_ref[...], (tc, LRU_WIDTH))
    for k in range(CONV_W):
        s0 = HALO - CONV_LEFT + k
        xc = xc + xpad[s0:s0 + tc, :] * cw_ref[k:k + 1, :]

    xcb = xc.astype(BF16)
    r = jax.nn.sigmoid(_dot(xcb, wa_ref[...]) + ba_ref[...])
    gi = jax.nn.sigmoid(_dot(xcb, wx_ref[...]) + bx_ref[...])
    z = -lam_ref[...]
    softplus = jnp.maximum(z, 0.0) + jnp.log1p(jnp.exp(-jnp.abs(z)))
    a = jnp.exp((-LRU_C * softplus) * r)
    u = jnp.sqrt(1.0 - a * a) * (gi * xc)

    sub = lax.broadcasted_iota(jnp.int32, (tc, LRU_WIDTH), 0) & (SUBLANES - 1)
    bv = u
    d = 1
    while d < SUBLANES:
        if reverse:
            shift, mask = tc - d, sub < SUBLANES - d
        else:
            shift, mask = d, sub >= d
        a_sh = pltpu.roll(a, shift, 0)
        b_sh = pltpu.roll(bv, shift, 0)
        bv = jnp.where(mask, a * b_sh + bv, bv)
        a = jnp.where(mask, a * a_sh, a)
        d *= 2
    a_s[...] = a
    b_s[...] = bv

    @pl.when(first)
    def _():
        carry[...] = jnp.zeros_like(carry)

    ntiles = tc // SUBLANES

    def body(i, cr):
        t = (ntiles - 1 - i) if reverse else i
        off = pl.multiple_of(t * SUBLANES, SUBLANES)
        ht = a_s[pl.ds(off, SUBLANES), :] * cr + b_s[pl.ds(off, SUBLANES), :]
        h_ref[pl.ds(off, SUBLANES), :] = ht
        edge = ht[0:1, :] if reverse else ht[SUBLANES - 1:SUBLANES, :]
        return jnp.broadcast_to(edge, (SUBLANES, LRU_WIDTH))

    carry[...] = lax.fori_loop(0, ntiles, body, carry[...])


def _lru(x, wl, d, S, tc, reverse):
    T = x.shape[0]
    nchunks = S // tc
    nblk = T // tc
    halo_per_chunk = tc // HALO
    nhalo = T // HALO
    if reverse:
        blk = lambda j: nblk - 1 - j
    else:
        blk = lambda j: j
    main = lambda j: (blk(j), 0)
    prev = lambda j: (jnp.maximum(blk(j) * halo_per_chunk - 1, 0), 0)
    nxt = lambda j: (jnp.minimum((blk(j) + 1) * halo_per_chunk, nhalo - 1), 0)
    const = lambda j: (0, 0)
    vec = pl.BlockSpec((1, LRU_WIDTH), const)
    mat = pl.BlockSpec((LRU_WIDTH, LRU_WIDTH), const)
    return pl.pallas_call(
        functools.partial(_lru_kernel, reverse=reverse, tc=tc, nchunks=nchunks, nblk=nblk),
        grid=(nblk,),
        in_specs=[
            pl.BlockSpec((tc, LRU_WIDTH), main),
            pl.BlockSpec((HALO, LRU_WIDTH), prev),
            pl.BlockSpec((HALO, LRU_WIDTH), nxt),
            pl.BlockSpec((CONV_W, LRU_WIDTH), const),
            vec, mat, vec, mat, vec, vec,
        ],
        out_specs=pl.BlockSpec((tc, LRU_WIDTH), main),
        out_shape=jax.ShapeDtypeStruct((T, LRU_WIDTH), F32),
        scratch_shapes=[
            pltpu.VMEM((tc + 2 * HALO, LRU_WIDTH), F32),
            pltpu.VMEM((tc, LRU_WIDTH), F32),
            pltpu.VMEM((tc, LRU_WIDTH), F32),
            pltpu.VMEM((SUBLANES, LRU_WIDTH), F32),
        ],
        compiler_params=pltpu.CompilerParams(dimension_semantics=("arbitrary",),
                                             vmem_limit_bytes=VMEM_LIMIT),
        name="lru_bwd" if reverse else "lru_fwd",
    )(x, x, x, wl["conv_w"], wl["conv_b"], wl["wa"][d], wl["ba"][d], wl["wx"][d], wl["bx"][d], wl["lam"][d])


def _attn_kernel(q_ref, kt_ref, v_ref, o_ref, *, tk, nk):
    q = q_ref[...]
    tq = q.shape[0]
    m = jnp.full((tq, 1), -jnp.inf, F32)
    acc = jnp.zeros((tq, HEAD_PAD), F32)
    s_next = _dot(q, kt_ref[0])
    for j in range(nk):
        s = s_next
        if j + 1 < nk:
            s_next = _dot(q, kt_ref[j + 1])
        m_new = jnp.maximum(m, jnp.max(s, axis=-1, keepdims=True))
        p = jnp.exp2(s - m_new)
        acc = acc * jnp.exp2(m - m_new) + _dot(p.astype(BF16), v_ref[j * tk:(j + 1) * tk, :])
        m = m_new
    lane = lax.broadcasted_iota(jnp.int32, acc.shape, 1)
    l = jnp.sum(jnp.where(lane == V_DIM, acc, 0.0), axis=-1, keepdims=True)
    o_ref[...] = (acc / l).astype(BF16)


def _attention(q, kt, v, B, S, tq, tk):
    T = q.shape[0]
    nq = S // tq
    nk = S // tk
    qmap = lambda b, h, i: (b * nq + i, h)
    return pl.pallas_call(
        functools.partial(_attn_kernel, tk=tk, nk=nk),
        grid=(B, MLA_HEADS, nq),
        in_specs=[
            pl.BlockSpec((tq, HEAD_PAD), qmap),
            pl.BlockSpec((nk, HEAD_PAD, tk), lambda b, h, i: (b, h, 0)),
            pl.BlockSpec((S, HEAD_PAD), lambda b, h, i: (b, h)),
        ],
        out_specs=pl.BlockSpec((tq, HEAD_PAD), qmap),
        out_shape=jax.ShapeDtypeStruct((T, MLA_PAD), BF16),
        compiler_params=pltpu.CompilerParams(dimension_semantics=("arbitrary", "arbitrary", "arbitrary"),
                                             vmem_limit_bytes=VMEM_LIMIT),
        name="attention",
    )(q, kt, v)


def _outproj_kernel(h_ref, hf_ref, hb_ref, gg_ref, o_ref, wl_ref, wm_ref, gffn_ref, h2_ref, hn2_ref):
    y = ((hf_ref[...] + hb_ref[...]) * gg_ref[...].astype(F32)).astype(BF16)
    h2 = h_ref[...] + _dot(y, wl_ref[...]) + _dot(o_ref[...], wm_ref[...])
    h2_ref[...] = h2
    hn2_ref[...] = _rms(h2, gffn_ref[...]).astype(BF16)


def _outproj(h, hf, hb, gg, o, wl, tm):
    T = h.shape[0]
    row = lambda i: (i, 0)
    const = lambda i: (0, 0)
    return pl.pallas_call(
        _outproj_kernel,
        grid=(T // tm,),
        in_specs=[
            pl.BlockSpec((tm, D_MODEL), row),
            pl.BlockSpec((tm, LRU_WIDTH), row),
            pl.BlockSpec((tm, LRU_WIDTH), row),
            pl.BlockSpec((tm, LRU_WIDTH), row),
            pl.BlockSpec((tm, MLA_PAD), row),
            pl.BlockSpec((LRU_WIDTH, D_MODEL), const),
            pl.BlockSpec((MLA_PAD, D_MODEL), const),
            pl.BlockSpec((1, D_MODEL), const),
        ],
        out_specs=[pl.BlockSpec((tm, D_MODEL), row), pl.BlockSpec((tm, D_MODEL), row)],
        out_shape=[jax.ShapeDtypeStruct((T, D_MODEL), F32), jax.ShapeDtypeStruct((T, D_MODEL), BF16)],
        compiler_params=pltpu.CompilerParams(dimension_semantics=("arbitrary",),
                                             vmem_limit_bytes=VMEM_LIMIT),
        name="outproj",
    )(h, hf, hb, gg, o, wl["w_out_lru"], wl["w_out_mla"], wl["g_ffn"])


FF_CHUNKS = 2
FF_CHUNK = D_FF // FF_CHUNKS
MXU_WIDTH = 256
FF_SUB = [(o, min(MXU_WIDTH, FF_CHUNK - o)) for o in range(0, FF_CHUNK, MXU_WIDTH)]


def _swiglu_rows(x, wg_ref, wu_ref, wd_ref, row_scale=None):
    y = None
    for off, w in FF_SUB:
        g = _dot(x, wg_ref[:, off:off + w])
        u = _dot(x, wu_ref[:, off:off + w])
        mid = (g * jax.nn.sigmoid(g)) * u
        if row_scale is not None:
            mid = mid * row_scale
        part = _dot(mid.astype(BF16), wd_ref[off:off + w, :])
        y = part if y is None else y + part
    return y


def _ple_epilogue(h3, p_ref, gple_ref, wpg_ref, wpp_ref, gfin_ref, final):
    gate = jax.nn.sigmoid(_dot(_rms(h3, gple_ref[...]).astype(BF16), wpg_ref[...]))
    h4 = h3 + _dot(p_ref[...].astype(BF16), wpp_ref[...]) * gate
    if final:
        h4 = _rms(h4, gfin_ref[...])
    return h4


def _ffn_kernel(hn_ref, h_ref, p_ref, wg_ref, wu_ref, wd_ref, gple_ref, wpg_ref, wpp_ref, gfin_ref, out_ref, acc,
                *, final):
    c = pl.program_id(1)

    @pl.when(c == 0)
    def _():
        acc[...] = jnp.zeros_like(acc)

    acc[...] += _swiglu_rows(hn_ref[...], wg_ref, wu_ref, wd_ref)

    @pl.when(c == FF_CHUNKS - 1)
    def _():
        out_ref[...] = _ple_epilogue(h_ref[...] + acc[...], p_ref, gple_ref, wpg_ref, wpp_ref, gfin_ref, final)


def _ffn(hn2, h2, p, l, wl, W, tm, final):
    T = h2.shape[0]
    jl = l // 2
    row = lambda i, c: (i, 0)
    const = lambda i, c: (0, 0)
    wg_spec = pl.BlockSpec((None, D_MODEL, FF_CHUNK), lambda i, c: (jl, 0, c))
    wd_spec = pl.BlockSpec((None, FF_CHUNK, D_MODEL), lambda i, c: (jl, c, 0))
    return pl.pallas_call(
        functools.partial(_ffn_kernel, final=final),
        grid=(T // tm, FF_CHUNKS),
        in_specs=[
            pl.BlockSpec((tm, D_MODEL), row),
            pl.BlockSpec((tm, D_MODEL), row),
            pl.BlockSpec((None, tm, D_PLE), lambda i, c: (l, i, 0)),
            wg_spec, wg_spec, wd_spec,
            pl.BlockSpec((1, D_MODEL), const),
            pl.BlockSpec((D_MODEL, D_MODEL), const),
            pl.BlockSpec((D_PLE, D_MODEL), const),
            pl.BlockSpec((1, D_MODEL), const),
        ],
        out_specs=pl.BlockSpec((tm, D_MODEL), row),
        out_shape=jax.ShapeDtypeStruct((T, D_MODEL), F32),
        scratch_shapes=[pltpu.VMEM((tm, D_MODEL), F32)],
        compiler_params=pltpu.CompilerParams(dimension_semantics=("arbitrary", "arbitrary"),
                                             vmem_limit_bytes=VMEM_LIMIT),
        name="ffn",
    )(hn2, h2, p, W["ff_wg"], W["ff_wu"], W["ff_wd"], wl["g_ple"], wl["ple_gate"], wl["ple_proj"],
      W["final_norm"])


ROW_TILE = 128
GATHER_SLAB = 256
META_LANES = 2 * N_EXPERTS


def _moe_rows(tb):
    return 2 * tb + N_EXPERTS * ROW_TILE


def _scatter_slab(tb):
    return 1024 if _moe_rows(tb) % 1024 == 0 else 512


def _router_kernel(h_ref, g_ref, wr_ref, route_ref, meta_ref):
    tb = h_ref.shape[0]
    x32 = _rms(h_ref[...], g_ref[...])
    logits = jnp.dot(x32, wr_ref[...], preferred_element_type=F32, precision=lax.Precision.HIGHEST)
    lane_i = lax.broadcasted_iota(jnp.int32, logits.shape, 1)
    lane = lane_i.astype(F32)
    neg = jnp.float32(-jnp.inf)
    big = jnp.float32(HEAD_PAD)
    logits = jnp.where(lane_i < N_EXPERTS, logits, neg)
    v1 = jnp.max(logits, axis=-1, keepdims=True)
    i1 = jnp.min(jnp.where(logits == v1, lane, big), axis=-1, keepdims=True)
    rest = jnp.where(lane == i1, neg, logits)
    v2 = jnp.max(rest, axis=-1, keepdims=True)
    i2 = jnp.min(jnp.where(rest == v2, lane, big), axis=-1, keepdims=True)
    e2 = jnp.exp(v2 - v1)
    w1 = 1.0 / (1.0 + e2)
    w2 = e2 / (1.0 + e2)

    sel = jnp.where((lane == i1) | (lane == i2), 1.0, 0.0)
    ri = lax.broadcasted_iota(jnp.int32, (tb, tb), 0)
    ci = lax.broadcasted_iota(jnp.int32, (tb, tb), 1)
    earlier = jnp.where(ci < ri, 1.0, 0.0).astype(BF16)
    rank = _dot(earlier, sel.astype(BF16))
    counts = jnp.sum(sel, axis=0, keepdims=True)
    ntiles = jnp.floor((counts + (ROW_TILE - 1)) * (1.0 / ROW_TILE))
    nt8 = jnp.broadcast_to(ntiles, (SUBLANES, HEAD_PAD)).astype(BF16)
    a = lax.broadcasted_iota(jnp.int32, (HEAD_PAD, HEAD_PAD), 0)
    b = lax.broadcasted_iota(jnp.int32, (HEAD_PAD, HEAD_PAD), 1)
    start = _dot(nt8, jnp.where(a < b, 1.0, 0.0).astype(BF16))
    shifted = (b >= N_EXPERTS) & (b < META_LANES) & (a < b - N_EXPERTS)
    start_hi = _dot(nt8, jnp.where(shifted, 1.0, 0.0).astype(BF16))
    base = start[0:1, :] * ROW_TILE + rank
    pos1 = jnp.sum(jnp.where(lane == i1, base, 0.0), axis=-1, keepdims=True)
    pos2 = jnp.sum(jnp.where(lane == i2, base, 0.0), axis=-1, keepdims=True)
    route_ref[...] = jnp.where(lane_i == 0, pos1, jnp.where(lane_i == 1, pos2,
                               jnp.where(lane_i == 2, w1, jnp.where(lane_i == 3, w2, 0.0))))
    meta_ref[...] = (jnp.broadcast_to(ntiles, (SUBLANES, HEAD_PAD)) + start_hi).astype(jnp.int32)


def _router(h2, wl, tb):
    T = h2.shape[0]
    nblk = T // tb
    return pl.pallas_call(
        _router_kernel,
        grid=(nblk,),
        in_specs=[
            pl.BlockSpec((tb, D_MODEL), lambda i: (i, 0)),
            pl.BlockSpec((1, D_MODEL), lambda i: (0, 0)),
            pl.BlockSpec((D_MODEL, HEAD_PAD), lambda i: (0, 0)),
        ],
        out_specs=[pl.BlockSpec((tb, HEAD_PAD), lambda i: (i, 0)),
                   pl.BlockSpec((SUBLANES, HEAD_PAD), lambda i: (i, 0))],
        out_shape=[jax.ShapeDtypeStruct((T, HEAD_PAD), F32),
                   jax.ShapeDtypeStruct((nblk * SUBLANES, HEAD_PAD), jnp.int32)],
        compiler_params=pltpu.CompilerParams(dimension_semantics=("arbitrary",),
                                             vmem_limit_bytes=VMEM_LIMIT),
        name="router",
    )(h2, wl["g_ffn"], wl["w_router"])


def _moe_kernel(meta_ref, hn_ref, route_ref, lanes_ref, wg_ref, wu_ref, wd_ref, out_ref, xs, ys, wrow):
    blk = pl.program_id(0)
    e = pl.program_id(1)
    c = pl.program_id(2)
    tb = hn_ref.shape[0]
    rows = xs.shape[0]
    slab_c = _scatter_slab(tb)
    m0 = blk * META_LANES
    last = N_EXPERTS - 1
    used = (meta_ref[m0 + N_EXPERTS + last] + meta_ref[m0 + last]) * ROW_TILE

    @pl.when((e == 0) & (c == 0))
    def _():
        ys[...] = jnp.zeros_like(ys)
        x = hn_ref[...]
        p1, p2 = lanes_ref[0:1, :], lanes_ref[1:2, :]
        w1, w2 = lanes_ref[2:3, :], lanes_ref[3:4, :]
        for r0 in range(0, rows, GATHER_SLAB):
            @pl.when(r0 < used)
            def _(r0=r0):
                rid = (lax.broadcasted_iota(jnp.int32, (GATHER_SLAB, tb), 0) + r0).astype(F32)
                hit1 = p1 == rid
                hit2 = p2 == rid
                onehot = jnp.where(hit1 | hit2, 1.0, 0.0).astype(BF16)
                xs[r0:r0 + GATHER_SLAB, :] = _dot(onehot, x).astype(BF16)
                wr = jnp.sum(jnp.where(hit1, w1, 0.0) + jnp.where(hit2, w2, 0.0), axis=-1, keepdims=True)
                wrow[r0:r0 + GATHER_SLAB, :] = jnp.broadcast_to(wr, (GATHER_SLAB, HEAD_PAD))

    ntile = meta_ref[m0 + e]
    first = meta_ref[m0 + N_EXPERTS + e]

    def body(i, carry):
        r = pl.multiple_of((first + i) * ROW_TILE, ROW_TILE)
        gate = wrow[pl.ds(r, ROW_TILE), :][:, 0:1]
        y = _swiglu_rows(xs[pl.ds(r, ROW_TILE), :], wg_ref, wu_ref, wd_ref, row_scale=gate)
        ys[pl.ds(r, ROW_TILE), :] = (ys[pl.ds(r, ROW_TILE), :].astype(F32) + y).astype(BF16)
        return carry

    lax.fori_loop(0, ntile, body, 0)

    @pl.when((e == last) & (c == FF_CHUNKS - 1))
    def _():
        p1c = route_ref[:, 0:1]
        p2c = route_ref[:, 1:2]
        out_ref[...] = jnp.zeros_like(out_ref)
        for r0 in range(0, rows, slab_c):
            @pl.when(r0 < used)
            def _(r0=r0):
                rid = (lax.broadcasted_iota(jnp.int32, (tb, slab_c), 1) + r0).astype(F32)
                onehot_t = jnp.where((p1c == rid) | (p2c == rid), 1.0, 0.0).astype(BF16)
                out_ref[...] += _dot(onehot_t, ys[r0:r0 + slab_c, :])


def _moe(hn2, route, meta, l, W, tb):
    T = hn2.shape[0]
    nblk = T // tb
    jl = l // 2
    rows = _moe_rows(tb)
    lanes = route[:, :SUBLANES].reshape(nblk, tb, SUBLANES).transpose(0, 2, 1)
    table = meta.reshape(nblk, SUBLANES, HEAD_PAD)[:, 0, :META_LANES].reshape(nblk * META_LANES)
    grid_spec = pltpu.PrefetchScalarGridSpec(
        num_scalar_prefetch=1,
        grid=(nblk, N_EXPERTS, FF_CHUNKS),
        in_specs=[
            pl.BlockSpec((tb, D_MODEL), lambda i, e, c, m: (i, 0)),
            pl.BlockSpec((tb, HEAD_PAD), lambda i, e, c, m: (i, 0)),
            pl.BlockSpec((None, SUBLANES, tb), lambda i, e, c, m: (i, 0, 0)),
            pl.BlockSpec((None, None, D_MODEL, FF_CHUNK), lambda i, e, c, m: (jl, e, 0, c)),
            pl.BlockSpec((None, None, D_MODEL, FF_CHUNK), lambda i, e, c, m: (jl, e, 0, c)),
            pl.BlockSpec((None, None, FF_CHUNK, D_MODEL), lambda i, e, c, m: (jl, e, c, 0)),
        ],
        out_specs=pl.BlockSpec((tb, D_MODEL), lambda i, e, c, m: (i, 0)),
        scratch_shapes=[pltpu.VMEM((rows, D_MODEL), BF16), pltpu.VMEM((rows, D_MODEL), BF16),
                        pltpu.VMEM((rows, HEAD_PAD), F32)],
    )
    return pl.pallas_call(
        _moe_kernel,
        grid_spec=grid_spec,
        out_shape=jax.ShapeDtypeStruct((T, D_MODEL), F32),
        compiler_params=pltpu.CompilerParams(dimension_semantics=("arbitrary", "arbitrary", "arbitrary"),
                                             vmem_limit_bytes=VMEM_LIMIT),
        name="moe",
    )(table, hn2, route, lanes, W["moe_wg"], W["moe_wu"], W["moe_wd"])


def _ple_kernel(h_ref, add_ref, p_ref, gple_ref, wpg_ref, wpp_ref, gfin_ref, out_ref, *, final):
    out_ref[...] = _ple_epilogue(h_ref[...] + add_ref[...], p_ref, gple_ref, wpg_ref, wpp_ref, gfin_ref, final)


def _ple(h2, add, p, l, wl, W, tm, final):
    T = h2.shape[0]
    row = lambda i: (i, 0)
    const = lambda i: (0, 0)
    return pl.pallas_call(
        functools.partial(_ple_kernel, final=final),
        grid=(T // tm,),
        in_specs=[
            pl.BlockSpec((tm, D_MODEL), row),
            pl.BlockSpec((tm, D_MODEL), row),
            pl.BlockSpec((None, tm, D_PLE), lambda i: (l, i, 0)),
            pl.BlockSpec((1, D_MODEL), const),
            pl.BlockSpec((D_MODEL, D_MODEL), const),
            pl.BlockSpec((D_PLE, D_MODEL), const),
            pl.BlockSpec((1, D_MODEL), const),
        ],
        out_specs=pl.BlockSpec((tm, D_MODEL), row),
        out_shape=jax.ShapeDtypeStruct((T, D_MODEL), F32),
        compiler_params=pltpu.CompilerParams(dimension_semantics=("arbitrary",),
                                             vmem_limit_bytes=VMEM_LIMIT),
        name="ple",
    )(h2, add, p, wl["g_ple"], wl["ple_gate"], wl["ple_proj"], W["final_norm"])


def _rot_half(w):
    half = QK_ROPE // 2
    return jnp.concatenate([-w[..., half:], w[..., :half]], axis=-1)


def _prepare_weights(norm_mix, w_in, conv_w, conv_b, lru_wa, lru_ba, lru_wx, lru_bx, lru_lambda, q_norm, w_uq,
                     kv_norm, w_ukv, w_out, norm_ffn, ff_wg, ff_wu, ff_wd, moe_router, moe_wg, moe_wu, moe_wd,
                     norm_ple, ple_gate, ple_proj, final_norm):
    eye = jnp.eye(LRU_BLOCKS, dtype=F32)
    layers = []
    for l in range(DEPTH):
        w_kpe = w_in[l][:, D_IN_MAIN:]
        zl = jnp.zeros((D_MODEL, QK_NOPE), F32)
        zr = jnp.zeros((D_MODEL, HEAD_PAD - QK_NOPE - QK_ROPE), F32)
        w_in_ext = jnp.concatenate([w_in[l][:, :D_IN_MAIN], zl, w_kpe, zr, zl, _rot_half(w_kpe), zr], axis=-1)
        uq = w_uq[l]
        zq = jnp.zeros((Q_RANK, MLA_HEADS, HEAD_PAD - QK_NOPE - QK_ROPE), F32)
        wqa = jnp.concatenate([uq, zq], axis=-1).reshape(Q_RANK, MLA_PAD)
        wqb = jnp.concatenate([jnp.zeros((Q_RANK, MLA_HEADS, QK_NOPE), F32), _rot_half(uq[..., QK_NOPE:]), zq],
                              axis=-1).reshape(Q_RANK, MLA_PAD)
        ukv = w_ukv[l]
        zk = jnp.zeros((KV_RANK, MLA_HEADS, HEAD_PAD - QK_NOPE), F32)
        wk = jnp.concatenate([ukv[..., :QK_NOPE], zk], axis=-1).reshape(KV_RANK, MLA_PAD)
        wv = jnp.concatenate([ukv[..., QK_NOPE:], zk], axis=-1).reshape(KV_RANK, MLA_PAD)
        wo_mla = w_out[l][LRU_WIDTH:].reshape(MLA_HEADS, V_DIM, D_MODEL)
        wo_mla = jnp.pad(wo_mla, ((0, 0), (0, HEAD_PAD - V_DIM), (0, 0))).reshape(MLA_PAD, D_MODEL)

        def blockdiag(w):
            return jnp.einsum("xncd,nm->xncmd", w, eye).reshape(2, LRU_WIDTH, LRU_WIDTH)

        w_router = jnp.pad(moe_router[l // 2], ((0, 0), (0, HEAD_PAD - N_EXPERTS))) if l % 2 == 1 else None
        layers.append(dict(
            g_mix=norm_mix[l][None], w_in=w_in_ext.astype(BF16),
            q_norm=q_norm[l][None], wqa=wqa.astype(BF16), wqb=wqb.astype(BF16),
            kv_norm=kv_norm[l][None], wk=wk.astype(BF16), wv=wv.astype(BF16),
            conv_w=conv_w[l], conv_b=conv_b[l][None],
            wa=blockdiag(lru_wa[l]).astype(BF16), ba=lru_ba[l][:, None, :],
            wx=blockdiag(lru_wx[l]).astype(BF16), bx=lru_bx[l][:, None, :],
            lam=lru_lambda[l][:, None, :],
            w_out_lru=w_out[l][:LRU_WIDTH].astype(BF16), w_out_mla=wo_mla.astype(BF16),
            g_ffn=norm_ffn[l][None], w_router=w_router,
            g_ple=norm_ple[l][None], ple_gate=ple_gate[l].astype(BF16), ple_proj=ple_proj[l].astype(BF16),
        ))
    W = dict(ff_wg=ff_wg.astype(BF16), ff_wu=ff_wu.astype(BF16), ff_wd=ff_wd.astype(BF16),
             moe_wg=moe_wg.astype(BF16), moe_wu=moe_wu.astype(BF16), moe_wd=moe_wd.astype(BF16),
             final_norm=final_norm[None])
    return layers, W


def _rope_tables(S):
    pos = jnp.arange(S, dtype=F32)
    inv = ROPE_THETA ** (-jnp.arange(0, QK_ROPE, 2, dtype=F32) / QK_ROPE)
    ang = pos[:, None] * inv[None, :]
    cos, sin = jnp.cos(ang), jnp.sin(ang)
    pad = jnp.zeros((S, HEAD_PAD - QK_NOPE - QK_ROPE), F32)
    cos_t = jnp.concatenate([jnp.ones((S, QK_NOPE), F32), cos, cos, pad], axis=-1)
    sin_t = jnp.concatenate([jnp.zeros((S, QK_NOPE), F32), sin, sin, pad], axis=-1)
    return cos_t, sin_t


def _tiles(B, S):
    T = B * S
    return dict(tm=min(512, S), tc=min(1024, S), tq=min(512, S), tf=min(512, T), tb=min(1024, T))


def _trunk(x, p, layers, W):
    B, S, _ = x.shape
    T = B * S
    t = _tiles(B, S)
    cos_t, sin_t = _rope_tables(S)
    h = x.reshape(T, D_MODEL)
    p = p.reshape(DEPTH, T, D_PLE)
    for l in range(DEPTH):
        wl = layers[l]
        final = (l == DEPTH - 1)
        xl, gg, q, kt, v = _inproj(h, wl, cos_t, sin_t, S, t["tm"])
        hf = _lru(xl, wl, 0, S, t["tc"], reverse=False)
        hb = _lru(xl, wl, 1, S, t["tc"], reverse=True)
        o = _attention(q, kt, v, B, S, t["tq"], t["tm"])
        h2, hn2 = _outproj(h, hf, hb, gg, o, wl, t["tm"])
        if l % 2 == 0:
            h = _ffn(hn2, h2, p, l, wl, W, t["tf"], final)
        else:
            route, meta = _router(h2, wl, t["tb"])
            h = _ple(h2, _moe(hn2, route, meta, l, W, t["tb"]), p, l, wl, W, t["tm"], final)
    return h.reshape(B, S, D_MODEL)


def kernel(x_prompt, x_sample, p_prompt, p_sample, norm_mix, w_in, conv_w, conv_b, lru_wa, lru_ba, lru_wx, lru_bx, lru_lambda, q_norm, w_uq, kv_norm, w_ukv, w_out, norm_ffn, ff_wg, ff_wu, ff_wd, moe_router, moe_wg, moe_wu, moe_wd, norm_ple, ple_gate, ple_proj, final_norm):
    layers, W = _prepare_weights(norm_mix, w_in, conv_w, conv_b, lru_wa, lru_ba, lru_wx, lru_bx, lru_lambda,
                                 q_norm, w_uq, kv_norm, w_ukv, w_out, norm_ffn, ff_wg, ff_wu, ff_wd, moe_router,
                                 moe_wg, moe_wu, moe_wd, norm_ple, ple_gate, ple_proj, final_norm)
    y_prompt = _trunk(x_prompt, p_prompt, layers, W)
    y_sample = _trunk(x_sample, p_sample, layers, W)
    return (y_prompt, y_sample)
```

```python
import functools
import math

import jax
import jax.numpy as jnp
from jax import lax
from jax.experimental import pallas as pl
from jax.experimental.pallas import tpu as pltpu

D_MODEL = 1024
DEPTH = 4
D_PLE = 256
LRU_WIDTH = 512
LRU_BLOCKS = 8
LRU_BLOCK_W = 64
CONV_W = 4
CONV_LEFT = 2
LRU_C = 8.0
MLA_HEADS = 8
QK_NOPE = 64
QK_ROPE = 32
V_DIM = 64
Q_RANK = 256
KV_RANK = 128
ROPE_THETA = 10000.0
D_FF = 2816
N_EXPERTS = 8
EPS = 1e-6

HEAD_PAD = 128
MLA_PAD = MLA_HEADS * HEAD_PAD
D_IN_MAIN = 2 * LRU_WIDTH + Q_RANK + KV_RANK
D_IN_EXT = D_IN_MAIN + 2 * HEAD_PAD
SUBLANES = 8
HALO = SUBLANES
VMEM_LIMIT = 56 * 1024 * 1024

F32 = jnp.float32
BF16 = jnp.bfloat16


def _rms(x, g):
    return x * lax.rsqrt(jnp.mean(x * x, axis=-1, keepdims=True) + EPS) * g


def _gelu_tanh(x):
    c = math.sqrt(2.0 / math.pi)
    return x * (0.5 * (1.0 + jnp.tanh(c * (x + 0.044715 * (x * x * x)))))


def _dot(a, b):
    return jnp.dot(a, b, preferred_element_type=F32)


def _inproj_kernel(h_ref, gmix_ref, win_ref, qn_ref, wqa_ref, wqb_ref, kvn_ref, wk_ref, wv_ref,
                   cos_ref, sin_ref, xl_ref, gg_ref, q_ref, kt_ref, v_ref):
    hn = _rms(h_ref[...], gmix_ref[...]).astype(BF16)
    proj = _dot(hn, win_ref[...])
    o1 = LRU_WIDTH
    o2 = 2 * LRU_WIDTH
    o3 = o2 + Q_RANK
    o4 = o3 + KV_RANK
    o5 = o4 + HEAD_PAD
    xl_ref[...] = proj[:, :o1]
    gg_ref[...] = _gelu_tanh(proj[:, o1:o2]).astype(BF16)
    cqn = _rms(proj[:, o2:o3], qn_ref[...]).astype(BF16)
    ckvn = _rms(proj[:, o3:o4], kvn_ref[...]).astype(BF16)
    cos = cos_ref[...]
    sin = sin_ref[...]
    kpe = proj[:, o4:o5] * cos + proj[:, o5:] * sin
    qa = _dot(cqn, wqa_ref[...])
    qb = _dot(cqn, wqb_ref[...])
    kk = _dot(ckvn, wk_ref[...])
    vv = _dot(ckvn, wv_ref[...])
    lane = lax.broadcasted_iota(jnp.int32, vv.shape, 1) & (HEAD_PAD - 1)
    v_ref[...] = jnp.where(lane == V_DIM, 1.0, vv).astype(BF16)
    scale = math.log2(math.e) / math.sqrt(QK_NOPE + QK_ROPE)
    for hd in range(MLA_HEADS):
        sl = slice(hd * HEAD_PAD, (hd + 1) * HEAD_PAD)
        q_ref[:, sl] = ((qa[:, sl] * cos + qb[:, sl] * sin) * scale).astype(BF16)
        kt_ref[0, sl, :] = (kk[:, sl] + kpe).T.astype(BF16)


def _inproj(h, wl, cos_t, sin_t, S, tm):
    T = h.shape[0]
    nblk_seq = S // tm
    row = lambda i: (i, 0)
    const = lambda i: (0, 0)
    pos = lambda i: (i % nblk_seq, 0)
    return pl.pallas_call(
        _inproj_kernel,
        grid=(T // tm,),
        in_specs=[
            pl.BlockSpec((tm, D_MODEL), row),
            pl.BlockSpec((1, D_MODEL), const),
            pl.BlockSpec((D_MODEL, D_IN_EXT), const),
            pl.BlockSpec((1, Q_RANK), const),
            pl.BlockSpec((Q_RANK, MLA_PAD), const),
            pl.BlockSpec((Q_RANK, MLA_PAD), const),
            pl.BlockSpec((1, KV_RANK), const),
            pl.BlockSpec((KV_RANK, MLA_PAD), const),
            pl.BlockSpec((KV_RANK, MLA_PAD), const),
            pl.BlockSpec((tm, HEAD_PAD), pos),
            pl.BlockSpec((tm, HEAD_PAD), pos),
        ],
        out_specs=[
            pl.BlockSpec((tm, LRU_WIDTH), row),
            pl.BlockSpec((tm, LRU_WIDTH), row),
            pl.BlockSpec((tm, MLA_PAD), row),
            pl.BlockSpec((1, MLA_PAD, tm), lambda i: (i, 0, 0)),
            pl.BlockSpec((tm, MLA_PAD), row),
        ],
        out_shape=[
            jax.ShapeDtypeStruct((T, LRU_WIDTH), F32),
            jax.ShapeDtypeStruct((T, LRU_WIDTH), BF16),
            jax.ShapeDtypeStruct((T, MLA_PAD), BF16),
            jax.ShapeDtypeStruct((T // tm, MLA_PAD, tm), BF16),
            jax.ShapeDtypeStruct((T, MLA_PAD), BF16),
        ],
        compiler_params=pltpu.CompilerParams(dimension_semantics=("arbitrary",),
                                             vmem_limit_bytes=VMEM_LIMIT),
        name="inproj",
    )(h, wl["g_mix"], wl["w_in"], wl["q_norm"], wl["wqa"], wl["wqb"], wl["kv_norm"], wl["wk"], wl["wv"],
      cos_t, sin_t)


def _lru_kernel(x_ref, xp_ref, xn_ref, cw_ref, cb_ref, wa_ref, ba_ref, wx_ref, bx_ref, lam_ref, h_ref,
                xpad, a_s, b_s, carry, *, reverse, tc, nchunks, nblk):
    j = pl.program_id(0)
    blk = (nblk - 1 - j) if reverse else j
    c = blk % nchunks
    first = (c == nchunks - 1) if reverse else (c == 0)

    xpad[0:HALO, :] = jnp.where(c > 0, xp_ref[...], 0.0)
    xpad[HALO:HALO + tc, :] = x_ref[...]
    xpad[HALO + tc:HALO + tc + HALO, :] = jnp.where(c < nchunks - 1, xn_ref[...], 0.0)
    xc = jnp.broadcast_to(cb_ref[...], (tc, LRU_WIDTH))
    for k in range(CONV_W):
        s0 = HALO - CONV_LEFT + k
        xc = xc + xpad[s0:s0 + tc, :] * cw_ref[k:k + 1, :]

    xcb = xc.astype(BF16)
    r = jax.nn.sigmoid(_dot(xcb, wa_ref[...]) + ba_ref[...])
    gi = jax.nn.sigmoid(_dot(xcb, wx_ref[...]) + bx_ref[...])
    z = -lam_ref[...]
    softplus = jnp.maximum(z, 0.0) + jnp.log1p(jnp.exp(-jnp.abs(z)))
    a = jnp.exp((-LRU_C * softplus) * r)
    u = jnp.sqrt(1.0 - a * a) * (gi * xc)

    sub = lax.broadcasted_iota(jnp.int32, (tc, LRU_WIDTH), 0) & (SUBLANES - 1)
    bv = u
    d = 1
    while d < SUBLANES:
        if reverse:
            shift, mask = tc - d, sub < SUBLANES - d
        else:
            shift, mask = d, sub >= d
        a_sh = pltpu.roll(a, shift, 0)
        b_sh = pltpu.roll(bv, shift, 0)
        bv = jnp.where(mask, a * b_sh + bv, bv)
        a = jnp.where(mask, a * a_sh, a)
        d *= 2
    a_s[...] = a
    b_s[...] = bv

    @pl.when(first)
    def _():
        carry[...] = jnp.zeros_like(carry)

    ntiles = tc // SUBLANES

    def body(i, cr):
        t = (ntiles - 1 - i) if reverse else i
        off = pl.multiple_of(t * SUBLANES, SUBLANES)
        ht = a_s[pl.ds(off, SUBLANES), :] * cr + b_s[pl.ds(off, SUBLANES), :]
        h_ref[pl.ds(off, SUBLANES), :] = ht
        edge = ht[0:1, :] if reverse else ht[SUBLANES - 1:SUBLANES, :]
        return jnp.broadcast_to(edge, (SUBLANES, LRU_WIDTH))

    carry[...] = lax.fori_loop(0, ntiles, body, carry[...])


def _lru(x, wl, d, S, tc, reverse):
    T = x.shape[0]
    nchunks = S // tc
    nblk = T // tc
    halo_per_chunk = tc // HALO
    nhalo = T // HALO
    if reverse:
        blk = lambda j: nblk - 1 - j
    else:
        blk = lambda j: j
    main = lambda j: (blk(j), 0)
    prev = lambda j: (jnp.maximum(blk(j) * halo_per_chunk - 1, 0), 0)
    nxt = lambda j: (jnp.minimum((blk(j) + 1) * halo_per_chunk, nhalo - 1), 0)
    const = lambda j: (0, 0)
    vec = pl.BlockSpec((1, LRU_WIDTH), const)
    mat = pl.BlockSpec((LRU_WIDTH, LRU_WIDTH), const)
    return pl.pallas_call(
        functools.partial(_lru_kernel, reverse=reverse, tc=tc, nchunks=nchunks, nblk=nblk),
        grid=(nblk,),
        in_specs=[
            pl.BlockSpec((tc, LRU_WIDTH), main),
            pl.BlockSpec((HALO, LRU_WIDTH), prev),
            pl.BlockSpec((HALO, LRU_WIDTH), nxt),
            pl.BlockSpec((CONV_W, LRU_WIDTH), const),
            vec, mat, vec, mat, vec, vec,
        ],
        out_specs=pl.BlockSpec((tc, LRU_WIDTH), main),
        out_shape=jax.ShapeDtypeStruct((T, LRU_WIDTH), F32),
        scratch_shapes=[
            pltpu.VMEM((tc + 2 * HALO, LRU_WIDTH), F32),
            pltpu.VMEM((tc, LRU_WIDTH), F32),
            pltpu.VMEM((tc, LRU_WIDTH), F32),
            pltpu.VMEM((SUBLANES, LRU_WIDTH), F32),
        ],
        compiler_params=pltpu.CompilerParams(dimension_semantics=("arbitrary",),
                                             vmem_limit_bytes=VMEM_LIMIT),
        name="lru_bwd" if reverse else "lru_fwd",
    )(x, x, x, wl["conv_w"], wl["conv_b"], wl["wa"][d], wl["ba"][d], wl["wx"][d], wl["bx"][d], wl["lam"][d])


def _attn_kernel(q_ref, kt_ref, v_ref, o_ref, *, tk, nk):
    q = q_ref[...]
    tq = q.shape[0]
    m = jnp.full((tq, 1), -jnp.inf, F32)
    acc = jnp.zeros((tq, HEAD_PAD), F32)
    s_next = _dot(q, kt_ref[0])
    for j in range(nk):
        s = s_next
        if j + 1 < nk:
            s_next = _dot(q, kt_ref[j + 1])
        m_new = jnp.maximum(m, jnp.max(s, axis=-1, keepdims=True))
        p = jnp.exp2(s - m_new)
        acc = acc * jnp.exp2(m - m_new) + _dot(p.astype(BF16), v_ref[j * tk:(j + 1) * tk, :])
        m = m_new
    lane = lax.broadcasted_iota(jnp.int32, acc.shape, 1)
    l = jnp.sum(jnp.where(lane == V_DIM, acc, 0.0), axis=-1, keepdims=True)
    o_ref[...] = (acc / l).astype(BF16)


def _attention(q, kt, v, B, S, tq, tk):
    T = q.shape[0]
    nq = S // tq
    nk = S // tk
    qmap = lambda b, h, i: (b * nq + i, h)
    return pl.pallas_call(
        functools.partial(_attn_kernel, tk=tk, nk=nk),
        grid=(B, MLA_HEADS, nq),
        in_specs=[
            pl.BlockSpec((tq, HEAD_PAD), qmap),
            pl.BlockSpec((nk, HEAD_PAD, tk), lambda b, h, i: (b, h, 0)),
            pl.BlockSpec((S, HEAD_PAD), lambda b, h, i: (b, h)),
        ],
        out_specs=pl.BlockSpec((tq, HEAD_PAD), qmap),
        out_shape=jax.ShapeDtypeStruct((T, MLA_PAD), BF16),
        compiler_params=pltpu.CompilerParams(dimension_semantics=("arbitrary", "arbitrary", "arbitrary"),
                                             vmem_limit_bytes=VMEM_LIMIT),
        name="attention",
    )(q, kt, v)


def _outproj_kernel(h_ref, hf_ref, hb_ref, gg_ref, o_ref, wl_ref, wm_ref, gffn_ref, h2_ref, hn2_ref):
    y = ((hf_ref[...] + hb_ref[...]) * gg_ref[...].astype(F32)).astype(BF16)
    h2 = h_ref[...] + _dot(y, wl_ref[...]) + _dot(o_ref[...], wm_ref[...])
    h2_ref[...] = h2
    hn2_ref[...] = _rms(h2, gffn_ref[...]).astype(BF16)


def _outproj(h, hf, hb, gg, o, wl, tm):
    T = h.shape[0]
    row = lambda i: (i, 0)
    const = lambda i: (0, 0)
    return pl.pallas_call(
        _outproj_kernel,
        grid=(T // tm,),
        in_specs=[
            pl.BlockSpec((tm, D_MODEL), row),
            pl.BlockSpec((tm, LRU_WIDTH), row),
            pl.BlockSpec((tm, LRU_WIDTH), row),
            pl.BlockSpec((tm, LRU_WIDTH), row),
            pl.BlockSpec((tm, MLA_PAD), row),
            pl.BlockSpec((LRU_WIDTH, D_MODEL), const),
            pl.BlockSpec((MLA_PAD, D_MODEL), const),
            pl.BlockSpec((1, D_MODEL), const),
        ],
        out_specs=[pl.BlockSpec((tm, D_MODEL), row), pl.BlockSpec((tm, D_MODEL), row)],
        out_shape=[jax.ShapeDtypeStruct((T, D_MODEL), F32), jax.ShapeDtypeStruct((T, D_MODEL), BF16)],
        compiler_params=pltpu.CompilerParams(dimension_semantics=("arbitrary",),
                                             vmem_limit_bytes=VMEM_LIMIT),
        name="outproj",
    )(h, hf, hb, gg, o, wl["w_out_lru"], wl["w_out_mla"], wl["g_ffn"])


FF_CHUNKS = 2
FF_CHUNK = D_FF // FF_CHUNKS
MXU_WIDTH = 256
FF_SUB = [(o, min(MXU_WIDTH, FF_CHUNK - o)) for o in range(0, FF_CHUNK, MXU_WIDTH)]


def _swiglu_rows(x, wg_ref, wu_ref, wd_ref, row_scale=None):
    y = None
    for off, w in FF_SUB:
        g = _dot(x, wg_ref[:, off:off + w])
        u = _dot(x, wu_ref[:, off:off + w])
        mid = (g * jax.nn.sigmoid(g)) * u
        if row_scale is not None:
            mid = mid * row_scale
        part = _dot(mid.astype(BF16), wd_ref[off:off + w, :])
        y = part if y is None else y + part
    return y


def _ple_epilogue(h3, p_ref, gple_ref, wpg_ref, wpp_ref, gfin_ref, final):
    gate = jax.nn.sigmoid(_dot(_rms(h3, gple_ref[...]).astype(BF16), wpg_ref[...]))
    h4 = h3 + _dot(p_ref[...].astype(BF16), wpp_ref[...]) * gate
    if final:
        h4 = _rms(h4, gfin_ref[...])
    return h4


def _ffn_kernel(hn_ref, h_ref, p_ref, wg_ref, wu_ref, wd_ref, gple_ref, wpg_ref, wpp_ref, gfin_ref, out_ref, acc,
                *, final):
    c = pl.program_id(1)

    @pl.when(c == 0)
    def _():
        acc[...] = jnp.zeros_like(acc)

    acc[...] += _swiglu_rows(hn_ref[...], wg_ref, wu_ref, wd_ref)

    @pl.when(c == FF_CHUNKS - 1)
    def _():
        out_ref[...] = _ple_epilogue(h_ref[...] + acc[...], p_ref, gple_ref, wpg_ref, wpp_ref, gfin_ref, final)


def _ffn(hn2, h2, p, l, wl, W, tm, final):
    T = h2.shape[0]
    jl = l // 2
    row = lambda i, c: (i, 0)
    const = lambda i, c: (0, 0)
    wg_spec = pl.BlockSpec((None, D_MODEL, FF_CHUNK), lambda i, c: (jl, 0, c))
    wd_spec = pl.BlockSpec((None, FF_CHUNK, D_MODEL), lambda i, c: (jl, c, 0))
    return pl.pallas_call(
        functools.partial(_ffn_kernel, final=final),
        grid=(T // tm, FF_CHUNKS),
        in_specs=[
            pl.BlockSpec((tm, D_MODEL), row),
            pl.BlockSpec((tm, D_MODEL), row),
            pl.BlockSpec((None, tm, D_PLE), lambda i, c: (l, i, 0)),
            wg_spec, wg_spec, wd_spec,
            pl.BlockSpec((1, D_MODEL), const),
            pl.BlockSpec((D_MODEL, D_MODEL), const),
            pl.BlockSpec((D_PLE, D_MODEL), const),
            pl.BlockSpec((1, D_MODEL), const),
        ],
        out_specs=pl.BlockSpec((tm, D_MODEL), row),
        out_shape=jax.ShapeDtypeStruct((T, D_MODEL), F32),
        scratch_shapes=[pltpu.VMEM((tm, D_MODEL), F32)],
        compiler_params=pltpu.CompilerParams(dimension_semantics=("arbitrary", "arbitrary"),
                                             vmem_limit_bytes=VMEM_LIMIT),
        name="ffn",
    )(hn2, h2, p, W["ff_wg"], W["ff_wu"], W["ff_wd"], wl["g_ple"], wl["ple_gate"], wl["ple_proj"],
      W["final_norm"])


ROW_TILE = 128
GATHER_SLAB = 256
EXPERT_TILE = 512
SEG_ALIGN = 16
GATE_LANES = HEAD_PAD
ROW_WIDTH = D_MODEL + GATE_LANES
TABLE_COLS = 3 * N_EXPERTS


def _moe_rows(tb):
    return 2 * tb + N_EXPERTS * ROW_TILE


def _scatter_slab(tb):
    return 1024 if _moe_rows(tb) % 1024 == 0 else 512


def _router_kernel(h_ref, g_ref, wr_ref, route_ref, meta_ref):
    tb = h_ref.shape[0]
    x32 = _rms(h_ref[...], g_ref[...])
    logits = jnp.dot(x32, wr_ref[...], preferred_element_type=F32, precision=lax.Precision.HIGHEST)
    lane_i = lax.broadcasted_iota(jnp.int32, logits.shape, 1)
    lane = lane_i.astype(F32)
    neg = jnp.float32(-jnp.inf)
    big = jnp.float32(HEAD_PAD)
    logits = jnp.where(lane_i < N_EXPERTS, logits, neg)
    v1 = jnp.max(logits, axis=-1, keepdims=True)
    i1 = jnp.min(jnp.where(logits == v1, lane, big), axis=-1, keepdims=True)
    rest = jnp.where(lane == i1, neg, logits)
    v2 = jnp.max(rest, axis=-1, keepdims=True)
    i2 = jnp.min(jnp.where(rest == v2, lane, big), axis=-1, keepdims=True)
    e2 = jnp.exp(v2 - v1)
    w1 = 1.0 / (1.0 + e2)
    w2 = e2 / (1.0 + e2)

    sel = jnp.where((lane == i1) | (lane == i2), 1.0, 0.0)
    ri = lax.broadcasted_iota(jnp.int32, (tb, tb), 0)
    ci = lax.broadcasted_iota(jnp.int32, (tb, tb), 1)
    earlier = jnp.where(ci < ri, 1.0, 0.0).astype(BF16)
    rank = _dot(earlier, sel.astype(BF16))
    counts = jnp.sum(sel, axis=0, keepdims=True)
    ntiles = jnp.floor((counts + (ROW_TILE - 1)) * (1.0 / ROW_TILE))
    nt8 = jnp.broadcast_to(ntiles, (SUBLANES, HEAD_PAD)).astype(BF16)
    a = lax.broadcasted_iota(jnp.int32, (HEAD_PAD, HEAD_PAD), 0)
    b = lax.broadcasted_iota(jnp.int32, (HEAD_PAD, HEAD_PAD), 1)
    start = _dot(nt8, jnp.where(a < b, 1.0, 0.0).astype(BF16))
    shifted = (b >= N_EXPERTS) & (b < 2 * N_EXPERTS) & (a < b - N_EXPERTS)
    start_hi = _dot(nt8, jnp.where(shifted, 1.0, 0.0).astype(BF16))
    base = start[0:1, :] * ROW_TILE + rank
    pos1 = jnp.sum(jnp.where(lane == i1, base, 0.0), axis=-1, keepdims=True)
    pos2 = jnp.sum(jnp.where(lane == i2, base, 0.0), axis=-1, keepdims=True)
    route_ref[...] = jnp.where(lane_i == 0, pos1, jnp.where(lane_i == 1, pos2,
                               jnp.where(lane_i == 2, w1, jnp.where(lane_i == 3, w2, 0.0))))
    sub = lax.broadcasted_iota(jnp.int32, (SUBLANES, HEAD_PAD), 0)
    tiles_row = jnp.broadcast_to(ntiles, (SUBLANES, HEAD_PAD)) + start_hi
    meta_ref[...] = jnp.where(sub == 1, jnp.broadcast_to(counts, (SUBLANES, HEAD_PAD)), tiles_row).astype(jnp.int32)


def _router(h2, wl, tb):
    T = h2.shape[0]
    nblk = T // tb
    return pl.pallas_call(
        _router_kernel,
        grid=(nblk,),
        in_specs=[
            pl.BlockSpec((tb, D_MODEL), lambda i: (i, 0)),
            pl.BlockSpec((1, D_MODEL), lambda i: (0, 0)),
            pl.BlockSpec((D_MODEL, HEAD_PAD), lambda i: (0, 0)),
        ],
        out_specs=[pl.BlockSpec((tb, HEAD_PAD), lambda i: (i, 0)),
                   pl.BlockSpec((SUBLANES, HEAD_PAD), lambda i: (i, 0))],
        out_shape=[jax.ShapeDtypeStruct((T, HEAD_PAD), F32),
                   jax.ShapeDtypeStruct((nblk * SUBLANES, HEAD_PAD), jnp.int32)],
        compiler_params=pltpu.CompilerParams(dimension_semantics=("arbitrary",),
                                             vmem_limit_bytes=VMEM_LIMIT),
        name="router",
    )(h2, wl["g_ffn"], wl["w_router"])


def _group_copies(tbl_ref, blk, local_ref, global_ref, sem, to_global):
    t0 = blk * TABLE_COLS

    def each(fn):
        for e in range(N_EXPERTS):
            ntile = tbl_ref[t0 + e]
            first = tbl_ref[t0 + N_EXPERTS + e]
            goff = tbl_ref[t0 + 2 * N_EXPERTS + e]

            def body(k, carry, first=first, goff=goff):
                loc = local_ref.at[pl.ds(pl.multiple_of((first + k) * ROW_TILE, ROW_TILE), ROW_TILE), :]
                glo = global_ref.at[pl.ds(pl.multiple_of(goff + k * ROW_TILE, SEG_ALIGN), ROW_TILE), :]
                fn(pltpu.make_async_copy(loc, glo, sem) if to_global else pltpu.make_async_copy(glo, loc, sem))
                return carry

            lax.fori_loop(0, ntile, body, 0)

    return each


def _used_rows(tbl_ref, blk):
    t0 = blk * TABLE_COLS
    last = N_EXPERTS - 1
    return (tbl_ref[t0 + N_EXPERTS + last] + tbl_ref[t0 + last]) * ROW_TILE


def _dispatch_kernel(tbl_ref, hn_ref, lanes_ref, xg_in_ref, xg_ref, xs, sem):
    del xg_in_ref
    blk = pl.program_id(0)
    tb = hn_ref.shape[0]
    rows = xs.shape[0]
    used = _used_rows(tbl_ref, blk)
    x = hn_ref[...]
    p1, p2 = lanes_ref[0:1, :], lanes_ref[1:2, :]
    w1, w2 = lanes_ref[2:3, :], lanes_ref[3:4, :]
    lane = lax.broadcasted_iota(jnp.int32, (GATHER_SLAB, GATE_LANES), 1)
    for r0 in range(0, rows, GATHER_SLAB):
        @pl.when(r0 < used)
        def _(r0=r0):
            rid = (lax.broadcasted_iota(jnp.int32, (GATHER_SLAB, tb), 0) + r0).astype(F32)
            hit1 = p1 == rid
            hit2 = p2 == rid
            onehot = jnp.where(hit1 | hit2, 1.0, 0.0).astype(BF16)
            xs[r0:r0 + GATHER_SLAB, :D_MODEL] = _dot(onehot, x).astype(BF16)
            wr = jnp.sum(jnp.where(hit1, w1, 0.0) + jnp.where(hit2, w2, 0.0), axis=-1, keepdims=True)
            hi = wr.astype(BF16).astype(F32)
            xs[r0:r0 + GATHER_SLAB, D_MODEL:] = jnp.where(lane == 0, hi, jnp.where(lane == 1, wr - hi, 0.0)
                                                           ).astype(BF16)

    copies = _group_copies(tbl_ref, blk, xs, xg_ref, sem, to_global=True)
    copies(lambda cp: cp.start())
    copies(lambda cp: cp.wait())


def _dispatch(hn2, lanes, table, zeros_rows, tb):
    T = hn2.shape[0]
    nblk = T // tb
    grid_spec = pltpu.PrefetchScalarGridSpec(
        num_scalar_prefetch=1,
        grid=(nblk,),
        in_specs=[
            pl.BlockSpec((tb, D_MODEL), lambda i, t: (i, 0)),
            pl.BlockSpec((None, SUBLANES, tb), lambda i, t: (i, 0, 0)),
            pl.BlockSpec(memory_space=pl.ANY),
        ],
        out_specs=pl.BlockSpec(memory_space=pl.ANY),
        scratch_shapes=[pltpu.VMEM((_moe_rows(tb), ROW_WIDTH), BF16), pltpu.SemaphoreType.DMA(())],
    )
    return pl.pallas_call(
        _dispatch_kernel,
        grid_spec=grid_spec,
        out_shape=jax.ShapeDtypeStruct(zeros_rows.shape, BF16),
        input_output_aliases={3: 0},
        compiler_params=pltpu.CompilerParams(dimension_semantics=("arbitrary",),
                                             vmem_limit_bytes=VMEM_LIMIT),
        name="dispatch",
    )(table, hn2, lanes, zeros_rows)


def _experts_kernel(te_ref, x_ref, wg_ref, wu_ref, wd_ref, y_ref, acc):
    t = pl.program_id(0)
    c = pl.program_id(1)
    ntile = pl.num_programs(0)
    live = t < te_ref[ntile]

    @pl.when(c == 0)
    def _():
        acc[...] = jnp.zeros_like(acc)

    @pl.when(live)
    def _():
        gate = x_ref[:, D_MODEL:D_MODEL + 1].astype(F32) + x_ref[:, D_MODEL + 1:D_MODEL + 2].astype(F32)
        acc[...] += _swiglu_rows(x_ref[:, :D_MODEL], wg_ref, wu_ref, wd_ref, row_scale=gate)

    @pl.when(c == FF_CHUNKS - 1)
    def _():
        y_ref[...] = acc[...].astype(BF16)


def _experts(xg, etable, l, W):
    rtot = xg.shape[0]
    jl = l // 2
    grid_spec = pltpu.PrefetchScalarGridSpec(
        num_scalar_prefetch=1,
        grid=(rtot // EXPERT_TILE, FF_CHUNKS),
        in_specs=[
            pl.BlockSpec((EXPERT_TILE, ROW_WIDTH), lambda t, c, te: (t, 0)),
            pl.BlockSpec((None, None, D_MODEL, FF_CHUNK), lambda t, c, te: (jl, te[t], 0, c)),
            pl.BlockSpec((None, None, D_MODEL, FF_CHUNK), lambda t, c, te: (jl, te[t], 0, c)),
            pl.BlockSpec((None, None, FF_CHUNK, D_MODEL), lambda t, c, te: (jl, te[t], c, 0)),
        ],
        out_specs=pl.BlockSpec((EXPERT_TILE, D_MODEL), lambda t, c, te: (t, 0)),
        scratch_shapes=[pltpu.VMEM((EXPERT_TILE, D_MODEL), F32)],
    )
    return pl.pallas_call(
        _experts_kernel,
        grid_spec=grid_spec,
        out_shape=jax.ShapeDtypeStruct((rtot, D_MODEL), BF16),
        compiler_params=pltpu.CompilerParams(dimension_semantics=("arbitrary", "arbitrary"),
                                             vmem_limit_bytes=VMEM_LIMIT),
        name="experts",
    )(etable, xg, W["moe_wg"], W["moe_wu"], W["moe_wd"])


def _collect_kernel(tbl_ref, route_ref, h_ref, p_ref, yg_ref, gple_ref, wpg_ref, wpp_ref, gfin_ref, out_ref,
                    ys, sem, *, final):
    blk = pl.program_id(0)
    tb = h_ref.shape[0]
    rows = ys.shape[0]
    slab = _scatter_slab(tb)
    used = _used_rows(tbl_ref, blk)
    copies = _group_copies(tbl_ref, blk, ys, yg_ref, sem, to_global=False)
    copies(lambda cp: cp.start())

    def clear(i, carry):
        ys[pl.ds(pl.multiple_of(i * ROW_TILE, ROW_TILE), ROW_TILE), :] = jnp.zeros((ROW_TILE, D_MODEL), BF16)
        return carry

    lax.fori_loop(used // ROW_TILE, rows // ROW_TILE, clear, 0)
    copies(lambda cp: cp.wait())

    p1c = route_ref[:, 0:1]
    p2c = route_ref[:, 1:2]
    out_ref[...] = h_ref[...]
    for r0 in range(0, rows, slab):
        @pl.when(r0 < used)
        def _(r0=r0):
            rid = (lax.broadcasted_iota(jnp.int32, (tb, slab), 1) + r0).astype(F32)
            onehot_t = jnp.where((p1c == rid) | (p2c == rid), 1.0, 0.0).astype(BF16)
            out_ref[...] += _dot(onehot_t, ys[r0:r0 + slab, :])
    out_ref[...] = _ple_epilogue(out_ref[...], p_ref, gple_ref, wpg_ref, wpp_ref, gfin_ref, final)


def _collect(route, h2, p, yg, table, l, wl, W, tb, final):
    T = h2.shape[0]
    row = lambda i, t: (i, 0)
    const = lambda i, t: (0, 0)
    grid_spec = pltpu.PrefetchScalarGridSpec(
        num_scalar_prefetch=1,
        grid=(T // tb,),
        in_specs=[
            pl.BlockSpec((tb, HEAD_PAD), row),
            pl.BlockSpec((tb, D_MODEL), row),
            pl.BlockSpec((None, tb, D_PLE), lambda i, t: (l, i, 0)),
            pl.BlockSpec(memory_space=pl.ANY),
            pl.BlockSpec((1, D_MODEL), const),
            pl.BlockSpec((D_MODEL, D_MODEL), const),
            pl.BlockSpec((D_PLE, D_MODEL), const),
            pl.BlockSpec((1, D_MODEL), const),
        ],
        out_specs=pl.BlockSpec((tb, D_MODEL), row),
        scratch_shapes=[pltpu.VMEM((_moe_rows(tb), D_MODEL), BF16), pltpu.SemaphoreType.DMA(())],
    )
    return pl.pallas_call(
        functools.partial(_collect_kernel, final=final),
        grid_spec=grid_spec,
        out_shape=jax.ShapeDtypeStruct((T, D_MODEL), F32),
        compiler_params=pltpu.CompilerParams(dimension_semantics=("arbitrary",),
                                             vmem_limit_bytes=VMEM_LIMIT),
        name="collect",
    )(table, route, h2, p, yg, wl["g_ple"], wl["ple_gate"], wl["ple_proj"], W["final_norm"])


def _global_rows(T, nblk):
    worst = 2 * T + (SEG_ALIGN - 1) * N_EXPERTS * nblk + N_EXPERTS * (ROW_TILE + EXPERT_TILE - 1)
    return -(-worst // EXPERT_TILE) * EXPERT_TILE


def _moe_plan(meta, T, nblk):
    m3 = meta.reshape(nblk, SUBLANES, HEAD_PAD)
    ntiles = m3[:, 0, :N_EXPERTS]
    first = m3[:, 0, N_EXPERTS:2 * N_EXPERTS]
    counts = m3[:, 1, :N_EXPERTS]
    seg = (counts + (SEG_ALIGN - 1)) // SEG_ALIGN * SEG_ALIGN
    region = (jnp.sum(seg, axis=0) + ROW_TILE + (EXPERT_TILE - 1)) // EXPERT_TILE * EXPERT_TILE
    region_end = jnp.cumsum(region)
    goff = (region_end - region)[None, :] + jnp.cumsum(seg, axis=0) - seg
    table = jnp.concatenate([ntiles, first, goff], axis=1).reshape(-1).astype(jnp.int32)
    ntile = _global_rows(T, nblk) // EXPERT_TILE
    tile_row = jnp.arange(ntile, dtype=jnp.int32) * EXPERT_TILE
    tile_expert = jnp.minimum(jnp.sum(tile_row[:, None] >= region_end[None, :], axis=1), N_EXPERTS - 1)
    etable = jnp.concatenate([tile_expert, region_end[-1:] // EXPERT_TILE]).astype(jnp.int32)
    return table, etable


def _moe_layer(hn2, h2, p, l, wl, W, tb, final):
    T = h2.shape[0]
    nblk = T // tb
    route, meta = _router(h2, wl, tb)
    table, etable = _moe_plan(meta, T, nblk)
    lanes = route[:, :SUBLANES].reshape(nblk, tb, SUBLANES).transpose(0, 2, 1)
    xg = _dispatch(hn2, lanes, table, jnp.zeros((_global_rows(T, nblk), ROW_WIDTH), BF16), tb)
    yg = _experts(xg, etable, l, W)
    return _collect(route, h2, p, yg, table, l, wl, W, tb, final)


def _rot_half(w):
    half = QK_ROPE // 2
    return jnp.concatenate([-w[..., half:], w[..., :half]], axis=-1)


def _prepare_weights(norm_mix, w_in, conv_w, conv_b, lru_wa, lru_ba, lru_wx, lru_bx, lru_lambda, q_norm, w_uq,
                     kv_norm, w_ukv, w_out, norm_ffn, ff_wg, ff_wu, ff_wd, moe_router, moe_wg, moe_wu, moe_wd,
                     norm_ple, ple_gate, ple_proj, final_norm):
    eye = jnp.eye(LRU_BLOCKS, dtype=F32)
    layers = []
    for l in range(DEPTH):
        w_kpe = w_in[l][:, D_IN_MAIN:]
        zl = jnp.zeros((D_MODEL, QK_NOPE), F32)
        zr = jnp.zeros((D_MODEL, HEAD_PAD - QK_NOPE - QK_ROPE), F32)
        w_in_ext = jnp.concatenate([w_in[l][:, :D_IN_MAIN], zl, w_kpe, zr, zl, _rot_half(w_kpe), zr], axis=-1)
        uq = w_uq[l]
        zq = jnp.zeros((Q_RANK, MLA_HEADS, HEAD_PAD - QK_NOPE - QK_ROPE), F32)
        wqa = jnp.concatenate([uq, zq], axis=-1).reshape(Q_RANK, MLA_PAD)
        wqb = jnp.concatenate([jnp.zeros((Q_RANK, MLA_HEADS, QK_NOPE), F32), _rot_half(uq[..., QK_NOPE:]), zq],
                              axis=-1).reshape(Q_RANK, MLA_PAD)
        ukv = w_ukv[l]
        zk = jnp.zeros((KV_RANK, MLA_HEADS, HEAD_PAD - QK_NOPE), F32)
        wk = jnp.concatenate([ukv[..., :QK_NOPE], zk], axis=-1).reshape(KV_RANK, MLA_PAD)
        wv = jnp.concatenate([ukv[..., QK_NOPE:], zk], axis=-1).reshape(KV_RANK, MLA_PAD)
        wo_mla = w_out[l][LRU_WIDTH:].reshape(MLA_HEADS, V_DIM, D_MODEL)
        wo_mla = jnp.pad(wo_mla, ((0, 0), (0, HEAD_PAD - V_DIM), (0, 0))).reshape(MLA_PAD, D_MODEL)

        def blockdiag(w):
            return jnp.einsum("xncd,nm->xncmd", w, eye).reshape(2, LRU_WIDTH, LRU_WIDTH)

        w_router = jnp.pad(moe_router[l // 2], ((0, 0), (0, HEAD_PAD - N_EXPERTS))) if l % 2 == 1 else None
        layers.append(dict(
            g_mix=norm_mix[l][None], w_in=w_in_ext.astype(BF16),
            q_norm=q_norm[l][None], wqa=wqa.astype(BF16), wqb=wqb.astype(BF16),
            kv_norm=kv_norm[l][None], wk=wk.astype(BF16), wv=wv.astype(BF16),
            conv_w=conv_w[l], conv_b=conv_b[l][None],
            wa=blockdiag(lru_wa[l]).astype(BF16), ba=lru_ba[l][:, None, :],
            wx=blockdiag(lru_wx[l]).astype(BF16), bx=lru_bx[l][:, None, :],
            lam=lru_lambda[l][:, None, :],
            w_out_lru=w_out[l][:LRU_WIDTH].astype(BF16), w_out_mla=wo_mla.astype(BF16),
            g_ffn=norm_ffn[l][None], w_router=w_router,
            g_ple=norm_ple[l][None], ple_gate=ple_gate[l].astype(BF16), ple_proj=ple_proj[l].astype(BF16),
        ))
    W = dict(ff_wg=ff_wg.astype(BF16), ff_wu=ff_wu.astype(BF16), ff_wd=ff_wd.astype(BF16),
             moe_wg=moe_wg.astype(BF16), moe_wu=moe_wu.astype(BF16), moe_wd=moe_wd.astype(BF16),
             final_norm=final_norm[None])
    return layers, W


def _rope_tables(S):
    pos = jnp.arange(S, dtype=F32)
    inv = ROPE_THETA ** (-jnp.arange(0, QK_ROPE, 2, dtype=F32) / QK_ROPE)
    ang = pos[:, None] * inv[None, :]
    cos, sin = jnp.cos(ang), jnp.sin(ang)
    pad = jnp.zeros((S, HEAD_PAD - QK_NOPE - QK_ROPE), F32)
    cos_t = jnp.concatenate([jnp.ones((S, QK_NOPE), F32), cos, cos, pad], axis=-1)
    sin_t = jnp.concatenate([jnp.zeros((S, QK_NOPE), F32), sin, sin, pad], axis=-1)
    return cos_t, sin_t


def _tiles(B, S):
    T = B * S
    return dict(tm=min(512, S), tc=min(1024, S), tq=min(512, S), tf=min(512, T), tb=min(1024, T))


def _trunk(x, p, layers, W):
    B, S, _ = x.shape
    T = B * S
    t = _tiles(B, S)
    cos_t, sin_t = _rope_tables(S)
    h = x.reshape(T, D_MODEL)
    p = p.reshape(DEPTH, T, D_PLE)
    for l in range(DEPTH):
        wl = layers[l]
        final = (l == DEPTH - 1)
        xl, gg, q, kt, v = _inproj(h, wl, cos_t, sin_t, S, t["tm"])
        hf = _lru(xl, wl, 0, S, t["tc"], reverse=False)
        hb = _lru(xl, wl, 1, S, t["tc"], reverse=True)
        o = _attention(q, kt, v, B, S, t["tq"], t["tm"])
        h2, hn2 = _outproj(h, hf, hb, gg, o, wl, t["tm"])
        if l % 2 == 0:
            h = _ffn(hn2, h2, p, l, wl, W, t["tf"], final)
        else:
            h = _moe_layer(hn2, h2, p, l, wl, W, t["tb"], final)
    return h.reshape(B, S, D_MODEL)


def kernel(x_prompt, x_sample, p_prompt, p_sample, norm_mix, w_in, conv_w, conv_b, lru_wa, lru_ba, lru_wx, lru_bx, lru_lambda, q_norm, w_uq, kv_norm, w_ukv, w_out, norm_ffn, ff_wg, ff_wu, ff_wd, moe_router, moe_wg, moe_wu, moe_wd, norm_ple, ple_gate, ple_proj, final_norm):
    layers, W = _prepare_weights(norm_mix, w_in, conv_w, conv_b, lru_wa, lru_ba, lru_wx, lru_bx, lru_lambda,
                                 q_norm, w_uq, kv_norm, w_ukv, w_out, norm_ffn, ff_wg, ff_wu, ff_wd, moe_router,
                                 moe_wg, moe_wu, moe_wd, norm_ple, ple_gate, ple_proj, final_norm)
    y_prompt = _trunk(x_prompt, p_prompt, layers, W)
    y_sample = _trunk(x_sample, p_sample, layers, W)
    return (y_prompt, y_sample)
```

```python
import functools
import math

import jax
import jax.numpy as jnp
from jax import lax
from jax.experimental import pallas as pl
from jax.experimental.pallas import tpu as pltpu

D_MODEL = 1024
DEPTH = 4
D_PLE = 256
LRU_WIDTH = 512
LRU_BLOCKS = 8
LRU_BLOCK_W = 64
CONV_W = 4
CONV_LEFT = 2
LRU_C = 8.0
MLA_HEADS = 8
QK_NOPE = 64
QK_ROPE = 32
V_DIM = 64
Q_RANK = 256
KV_RANK = 128
ROPE_THETA = 10000.0
D_FF = 2816
N_EXPERTS = 8
EPS = 1e-6

HEAD_PAD = 128
MLA_PAD = MLA_HEADS * HEAD_PAD
D_IN_MAIN = 2 * LRU_WIDTH + Q_RANK + KV_RANK
D_IN_EXT = D_IN_MAIN + 2 * HEAD_PAD
SUBLANES = 8
LANES = 128
LRU_LANE_BLOCKS = LRU_WIDTH // LANES
HALO = SUBLANES
VMEM_LIMIT = 56 * 1024 * 1024

F32 = jnp.float32
BF16 = jnp.bfloat16


def _rms(x, g):
    return x * lax.rsqrt(jnp.mean(x * x, axis=-1, keepdims=True) + EPS) * g


def _gelu_tanh(x):
    c = math.sqrt(2.0 / math.pi)
    return x * (0.5 * (1.0 + jnp.tanh(c * (x + 0.044715 * (x * x * x)))))


def _dot(a, b):
    return jnp.dot(a, b, preferred_element_type=F32)


def _inproj_kernel(h_ref, gmix_ref, win_ref, qn_ref, wqa_ref, wqb_ref, kvn_ref, wk_ref, wv_ref,
                   cos_ref, sin_ref, xl_ref, gg_ref, q_ref, kt_ref, v_ref):
    hn = _rms(h_ref[...], gmix_ref[...]).astype(BF16)
    proj = _dot(hn, win_ref[...])
    o1 = LRU_WIDTH
    o2 = 2 * LRU_WIDTH
    o3 = o2 + Q_RANK
    o4 = o3 + KV_RANK
    o5 = o4 + HEAD_PAD
    for cb in range(LRU_LANE_BLOCKS):
        xl_ref[cb] = proj[:, cb * LANES:(cb + 1) * LANES]
    gg_ref[...] = _gelu_tanh(proj[:, o1:o2]).astype(BF16)
    cqn = _rms(proj[:, o2:o3], qn_ref[...]).astype(BF16)
    ckvn = _rms(proj[:, o3:o4], kvn_ref[...]).astype(BF16)
    cos = cos_ref[...]
    sin = sin_ref[...]
    kpe = proj[:, o4:o5] * cos + proj[:, o5:] * sin
    qa = _dot(cqn, wqa_ref[...])
    qb = _dot(cqn, wqb_ref[...])
    kk = _dot(ckvn, wk_ref[...])
    vv = _dot(ckvn, wv_ref[...])
    lane = lax.broadcasted_iota(jnp.int32, vv.shape, 1) & (HEAD_PAD - 1)
    v_ref[...] = jnp.where(lane == V_DIM, 1.0, vv).astype(BF16)
    scale = math.log2(math.e) / math.sqrt(QK_NOPE + QK_ROPE)
    for hd in range(MLA_HEADS):
        sl = slice(hd * HEAD_PAD, (hd + 1) * HEAD_PAD)
        q_ref[:, sl] = ((qa[:, sl] * cos + qb[:, sl] * sin) * scale).astype(BF16)
        kt_ref[0, sl, :] = (kk[:, sl] + kpe).T.astype(BF16)


def _inproj(h, wl, cos_t, sin_t, S, tm):
    T = h.shape[0]
    nblk_seq = S // tm
    row = lambda i: (i, 0)
    const = lambda i: (0, 0)
    pos = lambda i: (i % nblk_seq, 0)
    return pl.pallas_call(
        _inproj_kernel,
        grid=(T // tm,),
        in_specs=[
            pl.BlockSpec((tm, D_MODEL), row),
            pl.BlockSpec((1, D_MODEL), const),
            pl.BlockSpec((D_MODEL, D_IN_EXT), const),
            pl.BlockSpec((1, Q_RANK), const),
            pl.BlockSpec((Q_RANK, MLA_PAD), const),
            pl.BlockSpec((Q_RANK, MLA_PAD), const),
            pl.BlockSpec((1, KV_RANK), const),
            pl.BlockSpec((KV_RANK, MLA_PAD), const),
            pl.BlockSpec((KV_RANK, MLA_PAD), const),
            pl.BlockSpec((tm, HEAD_PAD), pos),
            pl.BlockSpec((tm, HEAD_PAD), pos),
        ],
        out_specs=[
            pl.BlockSpec((LRU_LANE_BLOCKS, tm, LANES), lambda i: (0, i, 0)),
            pl.BlockSpec((tm, LRU_WIDTH), row),
            pl.BlockSpec((tm, MLA_PAD), row),
            pl.BlockSpec((1, MLA_PAD, tm), lambda i: (i, 0, 0)),
            pl.BlockSpec((tm, MLA_PAD), row),
        ],
        out_shape=[
            jax.ShapeDtypeStruct((LRU_LANE_BLOCKS, T, LANES), F32),
            jax.ShapeDtypeStruct((T, LRU_WIDTH), BF16),
            jax.ShapeDtypeStruct((T, MLA_PAD), BF16),
            jax.ShapeDtypeStruct((T // tm, MLA_PAD, tm), BF16),
            jax.ShapeDtypeStruct((T, MLA_PAD), BF16),
        ],
        compiler_params=pltpu.CompilerParams(dimension_semantics=("arbitrary",),
                                             vmem_limit_bytes=VMEM_LIMIT),
        name="inproj",
    )(h, wl["g_mix"], wl["w_in"], wl["q_norm"], wl["wqa"], wl["wqb"], wl["kv_norm"], wl["wk"], wl["wv"],
      cos_t, sin_t)


def _sigmoid(x):
    return 0.5 * jnp.tanh(0.5 * x) + 0.5


def _lru_kernel(x_ref, xp_ref, xn_ref, cw_ref, cb_ref, wa_ref, ba_ref, wx_ref, bx_ref, lam_ref, h_ref,
                xs, a_s, u_s, hl_s, ac_s, carry, *, reverse, tc, nchunks, nblk):
    j = pl.program_id(0)
    blk = (nblk - 1 - j) if reverse else j
    c = blk % nchunks
    first = (c == nchunks - 1) if reverse else (c == 0)
    seg = tc // SUBLANES
    for step in range(seg):
        for cb in range(LRU_LANE_BLOCKS):
            xs[step * SUBLANES:(step + 1) * SUBLANES, cb * LANES:(cb + 1) * LANES] = (
                x_ref[cb, pl.ds(step, SUBLANES, stride=seg), :])
    x = xs[...]

    sub = lax.broadcasted_iota(jnp.int32, (SUBLANES, LRU_WIDTH), 0)
    lane_cat = lambda ref: jnp.concatenate([ref[cb] for cb in range(LRU_LANE_BLOCKS)], axis=-1)
    xprev = jnp.where(c > 0, lane_cat(xp_ref), 0.0)
    xnext = jnp.where(c < nchunks - 1, lane_cat(xn_ref), 0.0)

    def from_prev_segment(tile, halo_row):
        return jnp.where(sub == 0, halo_row, pltpu.roll(tile, 1, 0))

    b1 = from_prev_segment(x[tc - SUBLANES:, :], xprev[HALO - 1:HALO, :])
    b2 = from_prev_segment(x[tc - 2 * SUBLANES:tc - SUBLANES, :], xprev[HALO - 2:HALO - 1, :])
    n1 = jnp.where(sub == SUBLANES - 1, xnext[0:1, :], pltpu.roll(x[:SUBLANES, :], SUBLANES - 1, 0))
    shifted = [jnp.concatenate([b2, b1, x[:tc - 2 * SUBLANES, :]], axis=0),
               jnp.concatenate([b1, x[:tc - SUBLANES, :]], axis=0),
               x,
               jnp.concatenate([x[SUBLANES:, :], n1], axis=0)]
    xc = jnp.broadcast_to(cb_ref[...], (tc, LRU_WIDTH))
    for k in range(CONV_W):
        xc = xc + shifted[k] * cw_ref[k:k + 1, :]

    xcb = xc.astype(BF16)
    r = _sigmoid(_dot(xcb, wa_ref[...]) + ba_ref[...])
    gi = _sigmoid(_dot(xcb, wx_ref[...]) + bx_ref[...])
    z = -lam_ref[...]
    softplus = jnp.maximum(z, 0.0) + jnp.log1p(jnp.exp(-jnp.abs(z)))
    a = jnp.exp((-LRU_C * softplus) * r)
    gap = 1.0 - a * a
    a_s[...] = a
    u_s[...] = jnp.where(gap > 0.0, gap * lax.rsqrt(gap), 0.0) * (gi * xc)

    def scan(i, state):
        h, prod = state
        step = (seg - 1 - i) if reverse else i
        rows = pl.ds(pl.multiple_of(step * SUBLANES, SUBLANES), SUBLANES)
        at = a_s[rows, :]
        h = at * h + u_s[rows, :]
        prod = at * prod
        hl_s[rows, :] = h
        ac_s[rows, :] = prod
        return h, prod

    zeros = jnp.zeros((SUBLANES, LRU_WIDTH), F32)
    h_end, a_end = lax.fori_loop(0, seg, scan, (zeros, zeros + 1.0), unroll=8)

    @pl.when(first)
    def _():
        carry[...] = jnp.zeros_like(carry)

    state = carry[0:1, :]
    entering = [None] * SUBLANES
    for s in (range(SUBLANES - 1, -1, -1) if reverse else range(SUBLANES)):
        entering[s] = state
        state = h_end[s:s + 1, :] + a_end[s:s + 1, :] * state
    carry[...] = jnp.broadcast_to(state, (SUBLANES, LRU_WIDTH))
    h_in = jnp.concatenate(entering, axis=0)

    for step in range(seg):
        rows = slice(step * SUBLANES, (step + 1) * SUBLANES)
        ht = hl_s[rows, :] + ac_s[rows, :] * h_in
        for cb in range(LRU_LANE_BLOCKS):
            h_ref[cb, pl.ds(step, SUBLANES, stride=seg), :] = ht[:, cb * LANES:(cb + 1) * LANES]


def _lru(x, wl, d, S, tc, reverse):
    T = x.shape[1]
    nchunks = S // tc
    nblk = T // tc
    halo_per_chunk = tc // HALO
    nhalo = T // HALO
    if reverse:
        blk = lambda j: nblk - 1 - j
    else:
        blk = lambda j: j
    main = lambda j: (0, blk(j), 0)
    prev = lambda j: (0, jnp.maximum(blk(j) * halo_per_chunk - 1, 0), 0)
    nxt = lambda j: (0, jnp.minimum((blk(j) + 1) * halo_per_chunk, nhalo - 1), 0)
    const = lambda j: (0, 0)
    vec = pl.BlockSpec((1, LRU_WIDTH), const)
    mat = pl.BlockSpec((LRU_WIDTH, LRU_WIDTH), const)
    return pl.pallas_call(
        functools.partial(_lru_kernel, reverse=reverse, tc=tc, nchunks=nchunks, nblk=nblk),
        grid=(nblk,),
        in_specs=[
            pl.BlockSpec((LRU_LANE_BLOCKS, tc, LANES), main),
            pl.BlockSpec((LRU_LANE_BLOCKS, HALO, LANES), prev),
            pl.BlockSpec((LRU_LANE_BLOCKS, HALO, LANES), nxt),
            pl.BlockSpec((CONV_W, LRU_WIDTH), const),
            vec, mat, vec, mat, vec, vec,
        ],
        out_specs=pl.BlockSpec((LRU_LANE_BLOCKS, tc, LANES), main),
        out_shape=jax.ShapeDtypeStruct((LRU_LANE_BLOCKS, T, LANES), F32),
        scratch_shapes=[pltpu.VMEM((tc, LRU_WIDTH), F32)] * 5 + [pltpu.VMEM((SUBLANES, LRU_WIDTH), F32)],
        compiler_params=pltpu.CompilerParams(dimension_semantics=("arbitrary",),
                                             vmem_limit_bytes=VMEM_LIMIT),
        name="lru_bwd" if reverse else "lru_fwd",
    )(x, x, x, wl["conv_w"], wl["conv_b"], wl["wa"][d], wl["ba"][d], wl["wx"][d], wl["bx"][d], wl["lam"][d])


def _attn_kernel(q_ref, kt_ref, v_ref, o_ref, *, tk, nk):
    q = q_ref[...]
    tq = q.shape[0]
    m = jnp.full((tq, 1), -jnp.inf, F32)
    acc = jnp.zeros((tq, HEAD_PAD), F32)
    s_next = _dot(q, kt_ref[0])
    for j in range(nk):
        s = s_next
        if j + 1 < nk:
            s_next = _dot(q, kt_ref[j + 1])
        m_new = jnp.maximum(m, jnp.max(s, axis=-1, keepdims=True))
        p = jnp.exp2(s - m_new)
        acc = acc * jnp.exp2(m - m_new) + _dot(p.astype(BF16), v_ref[j * tk:(j + 1) * tk, :])
        m = m_new
    lane = lax.broadcasted_iota(jnp.int32, acc.shape, 1)
    l = jnp.sum(jnp.where(lane == V_DIM, acc, 0.0), axis=-1, keepdims=True)
    o_ref[...] = (acc / l).astype(BF16)


def _attention(q, kt, v, B, S, tq, tk):
    T = q.shape[0]
    nq = S // tq
    nk = S // tk
    qmap = lambda b, h, i: (b * nq + i, h)
    return pl.pallas_call(
        functools.partial(_attn_kernel, tk=tk, nk=nk),
        grid=(B, MLA_HEADS, nq),
        in_specs=[
            pl.BlockSpec((tq, HEAD_PAD), qmap),
            pl.BlockSpec((nk, HEAD_PAD, tk), lambda b, h, i: (b, h, 0)),
            pl.BlockSpec((S, HEAD_PAD), lambda b, h, i: (b, h)),
        ],
        out_specs=pl.BlockSpec((tq, HEAD_PAD), qmap),
        out_shape=jax.ShapeDtypeStruct((T, MLA_PAD), BF16),
        compiler_params=pltpu.CompilerParams(dimension_semantics=("arbitrary", "arbitrary", "arbitrary"),
                                             vmem_limit_bytes=VMEM_LIMIT),
        name="attention",
    )(q, kt, v)


def _outproj_kernel(h_ref, hf_ref, hb_ref, gg_ref, o_ref, wl_ref, wm_ref, gffn_ref, h2_ref, hn2_ref):
    hsum = jnp.concatenate([hf_ref[cb] + hb_ref[cb] for cb in range(LRU_LANE_BLOCKS)], axis=-1)
    y = (hsum * gg_ref[...].astype(F32)).astype(BF16)
    h2 = h_ref[...] + _dot(y, wl_ref[...]) + _dot(o_ref[...], wm_ref[...])
    h2_ref[...] = h2
    hn2_ref[...] = _rms(h2, gffn_ref[...]).astype(BF16)


def _outproj(h, hf, hb, gg, o, wl, tm):
    T = h.shape[0]
    row = lambda i: (i, 0)
    const = lambda i: (0, 0)
    return pl.pallas_call(
        _outproj_kernel,
        grid=(T // tm,),
        in_specs=[
            pl.BlockSpec((tm, D_MODEL), row),
            pl.BlockSpec((LRU_LANE_BLOCKS, tm, LANES), lambda i: (0, i, 0)),
            pl.BlockSpec((LRU_LANE_BLOCKS, tm, LANES), lambda i: (0, i, 0)),
            pl.BlockSpec((tm, LRU_WIDTH), row),
            pl.BlockSpec((tm, MLA_PAD), row),
            pl.BlockSpec((LRU_WIDTH, D_MODEL), const),
            pl.BlockSpec((MLA_PAD, D_MODEL), const),
            pl.BlockSpec((1, D_MODEL), const),
        ],
        out_specs=[pl.BlockSpec((tm, D_MODEL), row), pl.BlockSpec((tm, D_MODEL), row)],
        out_shape=[jax.ShapeDtypeStruct((T, D_MODEL), F32), jax.ShapeDtypeStruct((T, D_MODEL), BF16)],
        compiler_params=pltpu.CompilerParams(dimension_semantics=("arbitrary",),
                                             vmem_limit_bytes=VMEM_LIMIT),
        name="outproj",
    )(h, hf, hb, gg, o, wl["w_out_lru"], wl["w_out_mla"], wl["g_ffn"])


FF_CHUNKS = 2
FF_CHUNK = D_FF // FF_CHUNKS
MXU_WIDTH = 256
FF_SUB = [(o, min(MXU_WIDTH, FF_CHUNK - o)) for o in range(0, FF_CHUNK, MXU_WIDTH)]


def _swiglu_rows(x, wg_ref, wu_ref, wd_ref, row_scale=None):
    y = None
    for off, w in FF_SUB:
        g = _dot(x, wg_ref[:, off:off + w])
        u = _dot(x, wu_ref[:, off:off + w])
        mid = (g * jax.nn.sigmoid(g)) * u
        if row_scale is not None:
            mid = mid * row_scale
        part = _dot(mid.astype(BF16), wd_ref[off:off + w, :])
        y = part if y is None else y + part
    return y


def _ple_epilogue(h3, p_ref, gple_ref, wpg_ref, wpp_ref, gfin_ref, final):
    gate = jax.nn.sigmoid(_dot(_rms(h3, gple_ref[...]).astype(BF16), wpg_ref[...]))
    h4 = h3 + _dot(p_ref[...].astype(BF16), wpp_ref[...]) * gate
    if final:
        h4 = _rms(h4, gfin_ref[...])
    return h4


def _ffn_kernel(hn_ref, h_ref, p_ref, wg_ref, wu_ref, wd_ref, gple_ref, wpg_ref, wpp_ref, gfin_ref, out_ref, acc,
                *, final):
    c = pl.program_id(1)

    @pl.when(c == 0)
    def _():
        acc[...] = jnp.zeros_like(acc)

    acc[...] += _swiglu_rows(hn_ref[...], wg_ref, wu_ref, wd_ref)

    @pl.when(c == FF_CHUNKS - 1)
    def _():
        out_ref[...] = _ple_epilogue(h_ref[...] + acc[...], p_ref, gple_ref, wpg_ref, wpp_ref, gfin_ref, final)


def _ffn(hn2, h2, p, l, wl, W, tm, final):
    T = h2.shape[0]
    jl = l // 2
    row = lambda i, c: (i, 0)
    const = lambda i, c: (0, 0)
    wg_spec = pl.BlockSpec((None, D_MODEL, FF_CHUNK), lambda i, c: (jl, 0, c))
    wd_spec = pl.BlockSpec((None, FF_CHUNK, D_MODEL), lambda i, c: (jl, c, 0))
    return pl.pallas_call(
        functools.partial(_ffn_kernel, final=final),
        grid=(T // tm, FF_CHUNKS),
        in_specs=[
            pl.BlockSpec((tm, D_MODEL), row),
            pl.BlockSpec((tm, D_MODEL), row),
            pl.BlockSpec((None, tm, D_PLE), lambda i, c: (l, i, 0)),
            wg_spec, wg_spec, wd_spec,
            pl.BlockSpec((1, D_MODEL), const),
            pl.BlockSpec((D_MODEL, D_MODEL), const),
            pl.BlockSpec((D_PLE, D_MODEL), const),
            pl.BlockSpec((1, D_MODEL), const),
        ],
        out_specs=pl.BlockSpec((tm, D_MODEL), row),
        out_shape=jax.ShapeDtypeStruct((T, D_MODEL), F32),
        scratch_shapes=[pltpu.VMEM((tm, D_MODEL), F32)],
        compiler_params=pltpu.CompilerParams(dimension_semantics=("arbitrary", "arbitrary"),
                                             vmem_limit_bytes=VMEM_LIMIT),
        name="ffn",
    )(hn2, h2, p, W["ff_wg"], W["ff_wu"], W["ff_wd"], wl["g_ple"], wl["ple_gate"], wl["ple_proj"],
      W["final_norm"])


ROW_TILE = 128
GATHER_SLAB = 256
EXPERT_TILE = 512
SEG_ALIGN = 16
GATE_LANES = HEAD_PAD
ROW_WIDTH = D_MODEL + GATE_LANES
TABLE_COLS = 3 * N_EXPERTS


def _moe_rows(tb):
    return 2 * tb + N_EXPERTS * ROW_TILE


def _scatter_slab(tb):
    return 1024 if _moe_rows(tb) % 1024 == 0 else 512


def _router_kernel(h_ref, g_ref, wr_ref, route_ref, meta_ref):
    tb = h_ref.shape[0]
    x32 = _rms(h_ref[...], g_ref[...])
    logits = jnp.dot(x32, wr_ref[...], preferred_element_type=F32, precision=lax.Precision.HIGHEST)
    lane_i = lax.broadcasted_iota(jnp.int32, logits.shape, 1)
    lane = lane_i.astype(F32)
    neg = jnp.float32(-jnp.inf)
    big = jnp.float32(HEAD_PAD)
    logits = jnp.where(lane_i < N_EXPERTS, logits, neg)
    v1 = jnp.max(logits, axis=-1, keepdims=True)
    i1 = jnp.min(jnp.where(logits == v1, lane, big), axis=-1, keepdims=True)
    rest = jnp.where(lane == i1, neg, logits)
    v2 = jnp.max(rest, axis=-1, keepdims=True)
    i2 = jnp.min(jnp.where(rest == v2, lane, big), axis=-1, keepdims=True)
    e2 = jnp.exp(v2 - v1)
    w1 = 1.0 / (1.0 + e2)
    w2 = e2 / (1.0 + e2)

    sel = jnp.where((lane == i1) | (lane == i2), 1.0, 0.0)
    ri = lax.broadcasted_iota(jnp.int32, (tb, tb), 0)
    ci = lax.broadcasted_iota(jnp.int32, (tb, tb), 1)
    earlier = jnp.where(ci < ri, 1.0, 0.0).astype(BF16)
    rank = _dot(earlier, sel.astype(BF16))
    counts = jnp.sum(sel, axis=0, keepdims=True)
    ntiles = jnp.floor((counts + (ROW_TILE - 1)) * (1.0 / ROW_TILE))
    nt8 = jnp.broadcast_to(ntiles, (SUBLANES, HEAD_PAD)).astype(BF16)
    a = lax.broadcasted_iota(jnp.int32, (HEAD_PAD, HEAD_PAD), 0)
    b = lax.broadcasted_iota(jnp.int32, (HEAD_PAD, HEAD_PAD), 1)
    start = _dot(nt8, jnp.where(a < b, 1.0, 0.0).astype(BF16))
    shifted = (b >= N_EXPERTS) & (b < 2 * N_EXPERTS) & (a < b - N_EXPERTS)
    start_hi = _dot(nt8, jnp.where(shifted, 1.0, 0.0).astype(BF16))
    base = start[0:1, :] * ROW_TILE + rank
    pos1 = jnp.sum(jnp.where(lane == i1, base, 0.0), axis=-1, keepdims=True)
    pos2 = jnp.sum(jnp.where(lane == i2, base, 0.0), axis=-1, keepdims=True)
    route_ref[...] = jnp.where(lane_i == 0, pos1, jnp.where(lane_i == 1, pos2,
                               jnp.where(lane_i == 2, w1, jnp.where(lane_i == 3, w2, 0.0))))
    sub = lax.broadcasted_iota(jnp.int32, (SUBLANES, HEAD_PAD), 0)
    tiles_row = jnp.broadcast_to(ntiles, (SUBLANES, HEAD_PAD)) + start_hi
    meta_ref[...] = jnp.where(sub == 1, jnp.broadcast_to(counts, (SUBLANES, HEAD_PAD)), tiles_row).astype(jnp.int32)


def _router(h2, wl, tb):
    T = h2.shape[0]
    nblk = T // tb
    return pl.pallas_call(
        _router_kernel,
        grid=(nblk,),
        in_specs=[
            pl.BlockSpec((tb, D_MODEL), lambda i: (i, 0)),
            pl.BlockSpec((1, D_MODEL), lambda i: (0, 0)),
            pl.BlockSpec((D_MODEL, HEAD_PAD), lambda i: (0, 0)),
        ],
        out_specs=[pl.BlockSpec((tb, HEAD_PAD), lambda i: (i, 0)),
                   pl.BlockSpec((SUBLANES, HEAD_PAD), lambda i: (i, 0))],
        out_shape=[jax.ShapeDtypeStruct((T, HEAD_PAD), F32),
                   jax.ShapeDtypeStruct((nblk * SUBLANES, HEAD_PAD), jnp.int32)],
        compiler_params=pltpu.CompilerParams(dimension_semantics=("arbitrary",),
                                             vmem_limit_bytes=VMEM_LIMIT),
        name="router",
    )(h2, wl["g_ffn"], wl["w_router"])


def _group_copies(tbl_ref, blk, local_ref, global_ref, sem, to_global):
    t0 = blk * TABLE_COLS

    def each(fn):
        for e in range(N_EXPERTS):
            ntile = tbl_ref[t0 + e]
            first = tbl_ref[t0 + N_EXPERTS + e]
            goff = tbl_ref[t0 + 2 * N_EXPERTS + e]

            def body(k, carry, first=first, goff=goff):
                loc = local_ref.at[pl.ds(pl.multiple_of((first + k) * ROW_TILE, ROW_TILE), ROW_TILE), :]
                glo = global_ref.at[pl.ds(pl.multiple_of(goff + k * ROW_TILE, SEG_ALIGN), ROW_TILE), :]
                fn(pltpu.make_async_copy(loc, glo, sem) if to_global else pltpu.make_async_copy(glo, loc, sem))
                return carry

            lax.fori_loop(0, ntile, body, 0)

    return each


def _used_rows(tbl_ref, blk):
    t0 = blk * TABLE_COLS
    last = N_EXPERTS - 1
    return (tbl_ref[t0 + N_EXPERTS + last] + tbl_ref[t0 + last]) * ROW_TILE


def _dispatch_kernel(tbl_ref, hn_ref, lanes_ref, xg_in_ref, xg_ref, xs, sem):
    del xg_in_ref
    blk = pl.program_id(0)
    tb = hn_ref.shape[0]
    rows = xs.shape[0]
    used = _used_rows(tbl_ref, blk)
    x = hn_ref[...]
    p1, p2 = lanes_ref[0:1, :], lanes_ref[1:2, :]
    w1, w2 = lanes_ref[2:3, :], lanes_ref[3:4, :]
    lane = lax.broadcasted_iota(jnp.int32, (GATHER_SLAB, GATE_LANES), 1)
    for r0 in range(0, rows, GATHER_SLAB):
        @pl.when(r0 < used)
        def _(r0=r0):
            rid = (lax.broadcasted_iota(jnp.int32, (GATHER_SLAB, tb), 0) + r0).astype(F32)
            hit1 = p1 == rid
            hit2 = p2 == rid
            onehot = jnp.where(hit1 | hit2, 1.0, 0.0).astype(BF16)
            xs[r0:r0 + GATHER_SLAB, :D_MODEL] = _dot(onehot, x).astype(BF16)
            wr = jnp.sum(jnp.where(hit1, w1, 0.0) + jnp.where(hit2, w2, 0.0), axis=-1, keepdims=True)
            hi = wr.astype(BF16).astype(F32)
            xs[r0:r0 + GATHER_SLAB, D_MODEL:] = jnp.where(lane == 0, hi, jnp.where(lane == 1, wr - hi, 0.0)
                                                           ).astype(BF16)

    copies = _group_copies(tbl_ref, blk, xs, xg_ref, sem, to_global=True)
    copies(lambda cp: cp.start())
    copies(lambda cp: cp.wait())


def _dispatch(hn2, lanes, table, zeros_rows, tb):
    T = hn2.shape[0]
    nblk = T // tb
    grid_spec = pltpu.PrefetchScalarGridSpec(
        num_scalar_prefetch=1,
        grid=(nblk,),
        in_specs=[
            pl.BlockSpec((tb, D_MODEL), lambda i, t: (i, 0)),
            pl.BlockSpec((None, SUBLANES, tb), lambda i, t: (i, 0, 0)),
            pl.BlockSpec(memory_space=pl.ANY),
        ],
        out_specs=pl.BlockSpec(memory_space=pl.ANY),
        scratch_shapes=[pltpu.VMEM((_moe_rows(tb), ROW_WIDTH), BF16), pltpu.SemaphoreType.DMA(())],
    )
    return pl.pallas_call(
        _dispatch_kernel,
        grid_spec=grid_spec,
        out_shape=jax.ShapeDtypeStruct(zeros_rows.shape, BF16),
        input_output_aliases={3: 0},
        compiler_params=pltpu.CompilerParams(dimension_semantics=("arbitrary",),
                                             vmem_limit_bytes=VMEM_LIMIT),
        name="dispatch",
    )(table, hn2, lanes, zeros_rows)


def _experts_kernel(te_ref, x_ref, wg_ref, wu_ref, wd_ref, y_ref, acc):
    t = pl.program_id(0)
    c = pl.program_id(1)
    ntile = pl.num_programs(0)
    live = t < te_ref[ntile]

    @pl.when(c == 0)
    def _():
        acc[...] = jnp.zeros_like(acc)

    @pl.when(live)
    def _():
        gate = x_ref[:, D_MODEL:D_MODEL + 1].astype(F32) + x_ref[:, D_MODEL + 1:D_MODEL + 2].astype(F32)
        acc[...] += _swiglu_rows(x_ref[:, :D_MODEL], wg_ref, wu_ref, wd_ref, row_scale=gate)

    @pl.when(c == FF_CHUNKS - 1)
    def _():
        y_ref[...] = acc[...].astype(BF16)


def _experts(xg, etable, l, W):
    rtot = xg.shape[0]
    jl = l // 2
    grid_spec = pltpu.PrefetchScalarGridSpec(
        num_scalar_prefetch=1,
        grid=(rtot // EXPERT_TILE, FF_CHUNKS),
        in_specs=[
            pl.BlockSpec((EXPERT_TILE, ROW_WIDTH), lambda t, c, te: (t, 0)),
            pl.BlockSpec((None, None, D_MODEL, FF_CHUNK), lambda t, c, te: (jl, te[t], 0, c)),
            pl.BlockSpec((None, None, D_MODEL, FF_CHUNK), lambda t, c, te: (jl, te[t], 0, c)),
            pl.BlockSpec((None, None, FF_CHUNK, D_MODEL), lambda t, c, te: (jl, te[t], c, 0)),
        ],
        out_specs=pl.BlockSpec((EXPERT_TILE, D_MODEL), lambda t, c, te: (t, 0)),
        scratch_shapes=[pltpu.VMEM((EXPERT_TILE, D_MODEL), F32)],
    )
    return pl.pallas_call(
        _experts_kernel,
        grid_spec=grid_spec,
        out_shape=jax.ShapeDtypeStruct((rtot, D_MODEL), BF16),
        compiler_params=pltpu.CompilerParams(dimension_semantics=("arbitrary", "arbitrary"),
                                             vmem_limit_bytes=VMEM_LIMIT),
        name="experts",
    )(etable, xg, W["moe_wg"], W["moe_wu"], W["moe_wd"])


def _collect_kernel(tbl_ref, route_ref, h_ref, p_ref, yg_ref, gple_ref, wpg_ref, wpp_ref, gfin_ref, out_ref,
                    ys, sem, *, final):
    blk = pl.program_id(0)
    tb = h_ref.shape[0]
    rows = ys.shape[0]
    slab = _scatter_slab(tb)
    used = _used_rows(tbl_ref, blk)
    copies = _group_copies(tbl_ref, blk, ys, yg_ref, sem, to_global=False)
    copies(lambda cp: cp.start())

    def clear(i, carry):
        ys[pl.ds(pl.multiple_of(i * ROW_TILE, ROW_TILE), ROW_TILE), :] = jnp.zeros((ROW_TILE, D_MODEL), BF16)
        return carry

    lax.fori_loop(used // ROW_TILE, rows // ROW_TILE, clear, 0)
    copies(lambda cp: cp.wait())

    p1c = route_ref[:, 0:1]
    p2c = route_ref[:, 1:2]
    out_ref[...] = h_ref[...]
    for r0 in range(0, rows, slab):
        @pl.when(r0 < used)
        def _(r0=r0):
            rid = (lax.broadcasted_iota(jnp.int32, (tb, slab), 1) + r0).astype(F32)
            onehot_t = jnp.where((p1c == rid) | (p2c == rid), 1.0, 0.0).astype(BF16)
            out_ref[...] += _dot(onehot_t, ys[r0:r0 + slab, :])
    out_ref[...] = _ple_epilogue(out_ref[...], p_ref, gple_ref, wpg_ref, wpp_ref, gfin_ref, final)


def _collect(route, h2, p, yg, table, l, wl, W, tb, final):
    T = h2.shape[0]
    row = lambda i, t: (i, 0)
    const = lambda i, t: (0, 0)
    grid_spec = pltpu.PrefetchScalarGridSpec(
        num_scalar_prefetch=1,
        grid=(T // tb,),
        in_specs=[
            pl.BlockSpec((tb, HEAD_PAD), row),
            pl.BlockSpec((tb, D_MODEL), row),
            pl.BlockSpec((None, tb, D_PLE), lambda i, t: (l, i, 0)),
            pl.BlockSpec(memory_space=pl.ANY),
            pl.BlockSpec((1, D_MODEL), const),
            pl.BlockSpec((D_MODEL, D_MODEL), const),
            pl.BlockSpec((D_PLE, D_MODEL), const),
            pl.BlockSpec((1, D_MODEL), const),
        ],
        out_specs=pl.BlockSpec((tb, D_MODEL), row),
        scratch_shapes=[pltpu.VMEM((_moe_rows(tb), D_MODEL), BF16), pltpu.SemaphoreType.DMA(())],
    )
    return pl.pallas_call(
        functools.partial(_collect_kernel, final=final),
        grid_spec=grid_spec,
        out_shape=jax.ShapeDtypeStruct((T, D_MODEL), F32),
        compiler_params=pltpu.CompilerParams(dimension_semantics=("arbitrary",),
                                             vmem_limit_bytes=VMEM_LIMIT),
        name="collect",
    )(table, route, h2, p, yg, wl["g_ple"], wl["ple_gate"], wl["ple_proj"], W["final_norm"])


def _global_rows(T, nblk):
    worst = 2 * T + (SEG_ALIGN - 1) * N_EXPERTS * nblk + N_EXPERTS * (ROW_TILE + EXPERT_TILE - 1)
    return -(-worst // EXPERT_TILE) * EXPERT_TILE


def _moe_plan(meta, T, nblk):
    m3 = meta.reshape(nblk, SUBLANES, HEAD_PAD)
    ntiles = m3[:, 0, :N_EXPERTS]
    first = m3[:, 0, N_EXPERTS:2 * N_EXPERTS]
    counts = m3[:, 1, :N_EXPERTS]
    seg = (counts + (SEG_ALIGN - 1)) // SEG_ALIGN * SEG_ALIGN
    region = (jnp.sum(seg, axis=0) + ROW_TILE + (EXPERT_TILE - 1)) // EXPERT_TILE * EXPERT_TILE
    region_end = jnp.cumsum(region)
    goff = (region_end - region)[None, :] + jnp.cumsum(seg, axis=0) - seg
    table = jnp.concatenate([ntiles, first, goff], axis=1).reshape(-1).astype(jnp.int32)
    ntile = _global_rows(T, nblk) // EXPERT_TILE
    tile_row = jnp.arange(ntile, dtype=jnp.int32) * EXPERT_TILE
    tile_expert = jnp.minimum(jnp.sum(tile_row[:, None] >= region_end[None, :], axis=1), N_EXPERTS - 1)
    etable = jnp.concatenate([tile_expert, region_end[-1:] // EXPERT_TILE]).astype(jnp.int32)
    return table, etable


def _moe_layer(hn2, h2, p, l, wl, W, tb, final):
    T = h2.shape[0]
    nblk = T // tb
    route, meta = _router(h2, wl, tb)
    table, etable = _moe_plan(meta, T, nblk)
    lanes = route[:, :SUBLANES].reshape(nblk, tb, SUBLANES).transpose(0, 2, 1)
    xg = _dispatch(hn2, lanes, table, jnp.zeros((_global_rows(T, nblk), ROW_WIDTH), BF16), tb)
    yg = _experts(xg, etable, l, W)
    return _collect(route, h2, p, yg, table, l, wl, W, tb, final)


def _rot_half(w):
    half = QK_ROPE // 2
    return jnp.concatenate([-w[..., half:], w[..., :half]], axis=-1)


def _prepare_weights(norm_mix, w_in, conv_w, conv_b, lru_wa, lru_ba, lru_wx, lru_bx, lru_lambda, q_norm, w_uq,
                     kv_norm, w_ukv, w_out, norm_ffn, ff_wg, ff_wu, ff_wd, moe_router, moe_wg, moe_wu, moe_wd,
                     norm_ple, ple_gate, ple_proj, final_norm):
    eye = jnp.eye(LRU_BLOCKS, dtype=F32)
    layers = []
    for l in range(DEPTH):
        w_kpe = w_in[l][:, D_IN_MAIN:]
        zl = jnp.zeros((D_MODEL, QK_NOPE), F32)
        zr = jnp.zeros((D_MODEL, HEAD_PAD - QK_NOPE - QK_ROPE), F32)
        w_in_ext = jnp.concatenate([w_in[l][:, :D_IN_MAIN], zl, w_kpe, zr, zl, _rot_half(w_kpe), zr], axis=-1)
        uq = w_uq[l]
        zq = jnp.zeros((Q_RANK, MLA_HEADS, HEAD_PAD - QK_NOPE - QK_ROPE), F32)
        wqa = jnp.concatenate([uq, zq], axis=-1).reshape(Q_RANK, MLA_PAD)
        wqb = jnp.concatenate([jnp.zeros((Q_RANK, MLA_HEADS, QK_NOPE), F32), _rot_half(uq[..., QK_NOPE:]), zq],
                              axis=-1).reshape(Q_RANK, MLA_PAD)
        ukv = w_ukv[l]
        zk = jnp.zeros((KV_RANK, MLA_HEADS, HEAD_PAD - QK_NOPE), F32)
        wk = jnp.concatenate([ukv[..., :QK_NOPE], zk], axis=-1).reshape(KV_RANK, MLA_PAD)
        wv = jnp.concatenate([ukv[..., QK_NOPE:], zk], axis=-1).reshape(KV_RANK, MLA_PAD)
        wo_mla = w_out[l][LRU_WIDTH:].reshape(MLA_HEADS, V_DIM, D_MODEL)
        wo_mla = jnp.pad(wo_mla, ((0, 0), (0, HEAD_PAD - V_DIM), (0, 0))).reshape(MLA_PAD, D_MODEL)

        def blockdiag(w):
            return jnp.einsum("xncd,nm->xncmd", w, eye).reshape(2, LRU_WIDTH, LRU_WIDTH)

        w_router = jnp.pad(moe_router[l // 2], ((0, 0), (0, HEAD_PAD - N_EXPERTS))) if l % 2 == 1 else None
        layers.append(dict(
            g_mix=norm_mix[l][None], w_in=w_in_ext.astype(BF16),
            q_norm=q_norm[l][None], wqa=wqa.astype(BF16), wqb=wqb.astype(BF16),
            kv_norm=kv_norm[l][None], wk=wk.astype(BF16), wv=wv.astype(BF16),
            conv_w=conv_w[l], conv_b=conv_b[l][None],
            wa=blockdiag(lru_wa[l]).astype(BF16), ba=lru_ba[l][:, None, :],
            wx=blockdiag(lru_wx[l]).astype(BF16), bx=lru_bx[l][:, None, :],
            lam=lru_lambda[l][:, None, :],
            w_out_lru=w_out[l][:LRU_WIDTH].astype(BF16), w_out_mla=wo_mla.astype(BF16),
            g_ffn=norm_ffn[l][None], w_router=w_router,
            g_ple=norm_ple[l][None], ple_gate=ple_gate[l].astype(BF16), ple_proj=ple_proj[l].astype(BF16),
        ))
    W = dict(ff_wg=ff_wg.astype(BF16), ff_wu=ff_wu.astype(BF16), ff_wd=ff_wd.astype(BF16),
             moe_wg=moe_wg.astype(BF16), moe_wu=moe_wu.astype(BF16), moe_wd=moe_wd.astype(BF16),
             final_norm=final_norm[None])
    return layers, W


def _rope_tables(S):
    pos = jnp.arange(S, dtype=F32)
    inv = ROPE_THETA ** (-jnp.arange(0, QK_ROPE, 2, dtype=F32) / QK_ROPE)
    ang = pos[:, None] * inv[None, :]
    cos, sin = jnp.cos(ang), jnp.sin(ang)
    pad = jnp.zeros((S, HEAD_PAD - QK_NOPE - QK_ROPE), F32)
    cos_t = jnp.concatenate([jnp.ones((S, QK_NOPE), F32), cos, cos, pad], axis=-1)
    sin_t = jnp.concatenate([jnp.zeros((S, QK_NOPE), F32), sin, sin, pad], axis=-1)
    return cos_t, sin_t


def _tiles(B, S):
    T = B * S
    return dict(tm=min(512, S), tc=min(1024, S), tq=min(512, S), tf=min(512, T), tb=min(1024, T))


def _trunk(x, p, layers, W):
    B, S, _ = x.shape
    T = B * S
    t = _tiles(B, S)
    cos_t, sin_t = _rope_tables(S)
    h = x.reshape(T, D_MODEL)
    p = p.reshape(DEPTH, T, D_PLE)
    for l in range(DEPTH):
        wl = layers[l]
        final = (l == DEPTH - 1)
        xl, gg, q, kt, v = _inproj(h, wl, cos_t, sin_t, S, t["tm"])
        hf = _lru(xl, wl, 0, S, t["tc"], reverse=False)
        hb = _lru(xl, wl, 1, S, t["tc"], reverse=True)
        o = _attention(q, kt, v, B, S, t["tq"], t["tm"])
        h2, hn2 = _outproj(h, hf, hb, gg, o, wl, t["tm"])
        if l % 2 == 0:
            h = _ffn(hn2, h2, p, l, wl, W, t["tf"], final)
        else:
            h = _moe_layer(hn2, h2, p, l, wl, W, t["tb"], final)
    return h.reshape(B, S, D_MODEL)


def kernel(x_prompt, x_sample, p_prompt, p_sample, norm_mix, w_in, conv_w, conv_b, lru_wa, lru_ba, lru_wx, lru_bx, lru_lambda, q_norm, w_uq, kv_norm, w_ukv, w_out, norm_ffn, ff_wg, ff_wu, ff_wd, moe_router, moe_wg, moe_wu, moe_wd, norm_ple, ple_gate, ple_proj, final_norm):
    layers, W = _prepare_weights(norm_mix, w_in, conv_w, conv_b, lru_wa, lru_ba, lru_wx, lru_bx, lru_lambda,
                                 q_norm, w_uq, kv_norm, w_ukv, w_out, norm_ffn, ff_wg, ff_wu, ff_wd, moe_router,
                                 moe_wg, moe_wu, moe_wd, norm_ple, ple_gate, ple_proj, final_norm)
    y_prompt = _trunk(x_prompt, p_prompt, layers, W)
    y_sample = _trunk(x_sample, p_sample, layers, W)
    return (y_prompt, y_sample)
```

```python
import functools
import math

import jax
import jax.numpy as jnp
from jax import lax
from jax.experimental import pallas as pl
from jax.experimental.pallas import tpu as pltpu

D_MODEL = 1024
DEPTH = 4
D_PLE = 256
LRU_WIDTH = 512
LRU_BLOCKS = 8
LRU_BLOCK_W = 64
CONV_W = 4
CONV_LEFT = 2
LRU_C = 8.0
MLA_HEADS = 8
QK_NOPE = 64
QK_ROPE = 32
V_DIM = 64
Q_RANK = 256
KV_RANK = 128
ROPE_THETA = 10000.0
D_FF = 2816
N_EXPERTS = 8
EPS = 1e-6

HEAD_PAD = 128
MLA_PAD = MLA_HEADS * HEAD_PAD
D_IN_MAIN = 2 * LRU_WIDTH + Q_RANK + KV_RANK
D_IN_EXT = D_IN_MAIN + 2 * HEAD_PAD
SUBLANES = 8
LANES = 128
LRU_LANE_BLOCKS = LRU_WIDTH // LANES
HALO = SUBLANES
VMEM_LIMIT = 56 * 1024 * 1024

F32 = jnp.float32
BF16 = jnp.bfloat16


def _rms(x, g):
    return x * lax.rsqrt(jnp.mean(x * x, axis=-1, keepdims=True) + EPS) * g


def _gelu_tanh(x):
    c = math.sqrt(2.0 / math.pi)
    return x * (0.5 * (1.0 + jnp.tanh(c * (x + 0.044715 * (x * x * x)))))


def _dot(a, b):
    return jnp.dot(a, b, preferred_element_type=F32)


def _inproj_kernel(h_ref, gmix_ref, win_ref, qn_ref, wqa_ref, wqb_ref, kvn_ref, wk_ref, wv_ref,
                   cos_ref, sin_ref, xl_ref, gg_ref, q_ref, kt_ref, v_ref):
    hn = _rms(h_ref[...], gmix_ref[...]).astype(BF16)
    proj = _dot(hn, win_ref[...])
    o1 = LRU_WIDTH
    o2 = 2 * LRU_WIDTH
    o3 = o2 + Q_RANK
    o4 = o3 + KV_RANK
    o5 = o4 + HEAD_PAD
    for cb in range(LRU_LANE_BLOCKS):
        xl_ref[cb] = proj[:, cb * LANES:(cb + 1) * LANES]
    gg_ref[...] = _gelu_tanh(proj[:, o1:o2]).astype(BF16)
    cqn = _rms(proj[:, o2:o3], qn_ref[...]).astype(BF16)
    ckvn = _rms(proj[:, o3:o4], kvn_ref[...]).astype(BF16)
    cos = cos_ref[...]
    sin = sin_ref[...]
    kpe = proj[:, o4:o5] * cos + proj[:, o5:] * sin
    qa = _dot(cqn, wqa_ref[...])
    qb = _dot(cqn, wqb_ref[...])
    kk = _dot(ckvn, wk_ref[...])
    vv = _dot(ckvn, wv_ref[...])
    lane = lax.broadcasted_iota(jnp.int32, vv.shape, 1) & (HEAD_PAD - 1)
    v_ref[...] = jnp.where(lane == V_DIM, 1.0, vv).astype(BF16)
    scale = math.log2(math.e) / math.sqrt(QK_NOPE + QK_ROPE)
    for hd in range(MLA_HEADS):
        sl = slice(hd * HEAD_PAD, (hd + 1) * HEAD_PAD)
        q_ref[:, sl] = ((qa[:, sl] * cos + qb[:, sl] * sin) * scale).astype(BF16)
        kt_ref[0, sl, :] = (kk[:, sl] + kpe).T.astype(BF16)


def _inproj(h, wl, cos_t, sin_t, S, tm):
    T = h.shape[0]
    nblk_seq = S // tm
    row = lambda i: (i, 0)
    const = lambda i: (0, 0)
    pos = lambda i: (i % nblk_seq, 0)
    return pl.pallas_call(
        _inproj_kernel,
        grid=(T // tm,),
        in_specs=[
            pl.BlockSpec((tm, D_MODEL), row),
            pl.BlockSpec((1, D_MODEL), const),
            pl.BlockSpec((D_MODEL, D_IN_EXT), const),
            pl.BlockSpec((1, Q_RANK), const),
            pl.BlockSpec((Q_RANK, MLA_PAD), const),
            pl.BlockSpec((Q_RANK, MLA_PAD), const),
            pl.BlockSpec((1, KV_RANK), const),
            pl.BlockSpec((KV_RANK, MLA_PAD), const),
            pl.BlockSpec((KV_RANK, MLA_PAD), const),
            pl.BlockSpec((tm, HEAD_PAD), pos),
            pl.BlockSpec((tm, HEAD_PAD), pos),
        ],
        out_specs=[
            pl.BlockSpec((LRU_LANE_BLOCKS, tm, LANES), lambda i: (0, i, 0)),
            pl.BlockSpec((tm, LRU_WIDTH), row),
            pl.BlockSpec((tm, MLA_PAD), row),
            pl.BlockSpec((1, MLA_PAD, tm), lambda i: (i, 0, 0)),
            pl.BlockSpec((tm, MLA_PAD), row),
        ],
        out_shape=[
            jax.ShapeDtypeStruct((LRU_LANE_BLOCKS, T, LANES), F32),
            jax.ShapeDtypeStruct((T, LRU_WIDTH), BF16),
            jax.ShapeDtypeStruct((T, MLA_PAD), BF16),
            jax.ShapeDtypeStruct((T // tm, MLA_PAD, tm), BF16),
            jax.ShapeDtypeStruct((T, MLA_PAD), BF16),
        ],
        compiler_params=pltpu.CompilerParams(dimension_semantics=("arbitrary",),
                                             vmem_limit_bytes=VMEM_LIMIT),
        name="inproj",
    )(h, wl["g_mix"], wl["w_in"], wl["q_norm"], wl["wqa"], wl["wqb"], wl["kv_norm"], wl["wk"], wl["wv"],
      cos_t, sin_t)


def _sigmoid(x):
    return 0.5 * jnp.tanh(0.5 * x) + 0.5


def _lru_kernel(x_ref, xp_ref, xn_ref, cw_ref, cb_ref, wa_ref, ba_ref, wx_ref, bx_ref, lam_ref, h_ref,
                xs, a_s, u_s, hl_s, ac_s, carry, *, reverse, tc, nchunks, nblk):
    j = pl.program_id(0)
    blk = (nblk - 1 - j) if reverse else j
    c = blk % nchunks
    first = (c == nchunks - 1) if reverse else (c == 0)
    seg = tc // SUBLANES
    for step in range(seg):
        for cb in range(LRU_LANE_BLOCKS):
            xs[step * SUBLANES:(step + 1) * SUBLANES, cb * LANES:(cb + 1) * LANES] = (
                x_ref[cb, pl.ds(step, SUBLANES, stride=seg), :])
    x = xs[...]

    sub = lax.broadcasted_iota(jnp.int32, (SUBLANES, LRU_WIDTH), 0)
    lane_cat = lambda ref: jnp.concatenate([ref[cb] for cb in range(LRU_LANE_BLOCKS)], axis=-1)
    xprev = jnp.where(c > 0, lane_cat(xp_ref), 0.0)
    xnext = jnp.where(c < nchunks - 1, lane_cat(xn_ref), 0.0)

    def from_prev_segment(tile, halo_row):
        return jnp.where(sub == 0, halo_row, pltpu.roll(tile, 1, 0))

    b1 = from_prev_segment(x[tc - SUBLANES:, :], xprev[HALO - 1:HALO, :])
    b2 = from_prev_segment(x[tc - 2 * SUBLANES:tc - SUBLANES, :], xprev[HALO - 2:HALO - 1, :])
    n1 = jnp.where(sub == SUBLANES - 1, xnext[0:1, :], pltpu.roll(x[:SUBLANES, :], SUBLANES - 1, 0))
    shifted = [jnp.concatenate([b2, b1, x[:tc - 2 * SUBLANES, :]], axis=0),
               jnp.concatenate([b1, x[:tc - SUBLANES, :]], axis=0),
               x,
               jnp.concatenate([x[SUBLANES:, :], n1], axis=0)]
    xc = jnp.broadcast_to(cb_ref[...], (tc, LRU_WIDTH))
    for k in range(CONV_W):
        xc = xc + shifted[k] * cw_ref[k:k + 1, :]

    xcb = xc.astype(BF16)
    r = _sigmoid(_dot(xcb, wa_ref[...]) + ba_ref[...])
    gi = _sigmoid(_dot(xcb, wx_ref[...]) + bx_ref[...])
    z = -lam_ref[...]
    softplus = jnp.maximum(z, 0.0) + jnp.log1p(jnp.exp(-jnp.abs(z)))
    a = jnp.exp((-LRU_C * softplus) * r)
    gap = 1.0 - a * a
    a_s[...] = a
    u_s[...] = jnp.where(gap > 0.0, gap * lax.rsqrt(gap), 0.0) * (gi * xc)

    def scan(i, state):
        h, prod = state
        step = (seg - 1 - i) if reverse else i
        rows = pl.ds(pl.multiple_of(step * SUBLANES, SUBLANES), SUBLANES)
        at = a_s[rows, :]
        h = at * h + u_s[rows, :]
        prod = at * prod
        hl_s[rows, :] = h
        ac_s[rows, :] = prod
        return h, prod

    zeros = jnp.zeros((SUBLANES, LRU_WIDTH), F32)
    h_end, a_end = lax.fori_loop(0, seg, scan, (zeros, zeros + 1.0), unroll=8)

    @pl.when(first)
    def _():
        carry[...] = jnp.zeros_like(carry)

    state = carry[0:1, :]
    entering = [None] * SUBLANES
    for s in (range(SUBLANES - 1, -1, -1) if reverse else range(SUBLANES)):
        entering[s] = state
        state = h_end[s:s + 1, :] + a_end[s:s + 1, :] * state
    carry[...] = jnp.broadcast_to(state, (SUBLANES, LRU_WIDTH))
    h_in = jnp.concatenate(entering, axis=0)

    for step in range(seg):
        rows = slice(step * SUBLANES, (step + 1) * SUBLANES)
        ht = hl_s[rows, :] + ac_s[rows, :] * h_in
        for cb in range(LRU_LANE_BLOCKS):
            h_ref[cb, pl.ds(step, SUBLANES, stride=seg), :] = ht[:, cb * LANES:(cb + 1) * LANES]


def _lru(x, wl, d, S, tc, reverse):
    T = x.shape[1]
    nchunks = S // tc
    nblk = T // tc
    halo_per_chunk = tc // HALO
    nhalo = T // HALO
    if reverse:
        blk = lambda j: nblk - 1 - j
    else:
        blk = lambda j: j
    main = lambda j: (0, blk(j), 0)
    prev = lambda j: (0, jnp.maximum(blk(j) * halo_per_chunk - 1, 0), 0)
    nxt = lambda j: (0, jnp.minimum((blk(j) + 1) * halo_per_chunk, nhalo - 1), 0)
    const = lambda j: (0, 0)
    vec = pl.BlockSpec((1, LRU_WIDTH), const)
    mat = pl.BlockSpec((LRU_WIDTH, LRU_WIDTH), const)
    return pl.pallas_call(
        functools.partial(_lru_kernel, reverse=reverse, tc=tc, nchunks=nchunks, nblk=nblk),
        grid=(nblk,),
        in_specs=[
            pl.BlockSpec((LRU_LANE_BLOCKS, tc, LANES), main),
            pl.BlockSpec((LRU_LANE_BLOCKS, HALO, LANES), prev),
            pl.BlockSpec((LRU_LANE_BLOCKS, HALO, LANES), nxt),
            pl.BlockSpec((CONV_W, LRU_WIDTH), const),
            vec, mat, vec, mat, vec, vec,
        ],
        out_specs=pl.BlockSpec((LRU_LANE_BLOCKS, tc, LANES), main),
        out_shape=jax.ShapeDtypeStruct((LRU_LANE_BLOCKS, T, LANES), F32),
        scratch_shapes=[pltpu.VMEM((tc, LRU_WIDTH), F32)] * 5 + [pltpu.VMEM((SUBLANES, LRU_WIDTH), F32)],
        compiler_params=pltpu.CompilerParams(dimension_semantics=("arbitrary",),
                                             vmem_limit_bytes=VMEM_LIMIT),
        name="lru_bwd" if reverse else "lru_fwd",
    )(x, x, x, wl["conv_w"], wl["conv_b"], wl["wa"][d], wl["ba"][d], wl["wx"][d], wl["bx"][d], wl["lam"][d])


def _attn_kernel(q_ref, kt_ref, v_ref, o_ref, *, tk, nk):
    q = q_ref[...]
    tq = q.shape[0]
    m = jnp.full((tq, 1), -jnp.inf, F32)
    acc = jnp.zeros((tq, HEAD_PAD), F32)
    s_next = _dot(q, kt_ref[0])
    for j in range(nk):
        s = s_next
        if j + 1 < nk:
            s_next = _dot(q, kt_ref[j + 1])
        m_new = jnp.maximum(m, jnp.max(s, axis=-1, keepdims=True))
        p = jnp.exp2(s - m_new)
        acc = acc * jnp.exp2(m - m_new) + _dot(p.astype(BF16), v_ref[j * tk:(j + 1) * tk, :])
        m = m_new
    lane = lax.broadcasted_iota(jnp.int32, acc.shape, 1)
    l = jnp.sum(jnp.where(lane == V_DIM, acc, 0.0), axis=-1, keepdims=True)
    o_ref[...] = (acc / l).astype(BF16)


def _attention(q, kt, v, B, S, tq, tk):
    T = q.shape[0]
    nq = S // tq
    nk = S // tk
    qmap = lambda b, h, i: (b * nq + i, h)
    return pl.pallas_call(
        functools.partial(_attn_kernel, tk=tk, nk=nk),
        grid=(B, MLA_HEADS, nq),
        in_specs=[
            pl.BlockSpec((tq, HEAD_PAD), qmap),
            pl.BlockSpec((nk, HEAD_PAD, tk), lambda b, h, i: (b, h, 0)),
            pl.BlockSpec((S, HEAD_PAD), lambda b, h, i: (b, h)),
        ],
        out_specs=pl.BlockSpec((tq, HEAD_PAD), qmap),
        out_shape=jax.ShapeDtypeStruct((T, MLA_PAD), BF16),
        compiler_params=pltpu.CompilerParams(dimension_semantics=("arbitrary", "arbitrary", "arbitrary"),
                                             vmem_limit_bytes=VMEM_LIMIT),
        name="attention",
    )(q, kt, v)


def _outproj_kernel(h_ref, hf_ref, hb_ref, gg_ref, o_ref, wl_ref, wm_ref, gffn_ref, h2_ref, hn2_ref):
    hsum = jnp.concatenate([hf_ref[cb] + hb_ref[cb] for cb in range(LRU_LANE_BLOCKS)], axis=-1)
    y = (hsum * gg_ref[...].astype(F32)).astype(BF16)
    h2 = h_ref[...] + _dot(y, wl_ref[...]) + _dot(o_ref[...], wm_ref[...])
    h2_ref[...] = h2
    hn2_ref[...] = _rms(h2, gffn_ref[...]).astype(BF16)


def _outproj(h, hf, hb, gg, o, wl, tm):
    T = h.shape[0]
    row = lambda i: (i, 0)
    const = lambda i: (0, 0)
    return pl.pallas_call(
        _outproj_kernel,
        grid=(T // tm,),
        in_specs=[
            pl.BlockSpec((tm, D_MODEL), row),
            pl.BlockSpec((LRU_LANE_BLOCKS, tm, LANES), lambda i: (0, i, 0)),
            pl.BlockSpec((LRU_LANE_BLOCKS, tm, LANES), lambda i: (0, i, 0)),
            pl.BlockSpec((tm, LRU_WIDTH), row),
            pl.BlockSpec((tm, MLA_PAD), row),
            pl.BlockSpec((LRU_WIDTH, D_MODEL), const),
            pl.BlockSpec((MLA_PAD, D_MODEL), const),
            pl.BlockSpec((1, D_MODEL), const),
        ],
        out_specs=[pl.BlockSpec((tm, D_MODEL), row), pl.BlockSpec((tm, D_MODEL), row)],
        out_shape=[jax.ShapeDtypeStruct((T, D_MODEL), F32), jax.ShapeDtypeStruct((T, D_MODEL), BF16)],
        compiler_params=pltpu.CompilerParams(dimension_semantics=("arbitrary",),
                                             vmem_limit_bytes=VMEM_LIMIT),
        name="outproj",
    )(h, hf, hb, gg, o, wl["w_out_lru"], wl["w_out_mla"], wl["g_ffn"])


FF_CHUNKS = 2
FF_CHUNK = D_FF // FF_CHUNKS
MXU_WIDTH = 256
FF_SUB = [(o, min(MXU_WIDTH, FF_CHUNK - o)) for o in range(0, FF_CHUNK, MXU_WIDTH)]


def _swiglu_rows(x, wg_ref, wu_ref, wd_ref, row_scale=None):
    y = None
    for off, w in FF_SUB:
        g = _dot(x, wg_ref[:, off:off + w])
        u = _dot(x, wu_ref[:, off:off + w])
        mid = (g * jax.nn.sigmoid(g)) * u
        if row_scale is not None:
            mid = mid * row_scale
        part = _dot(mid.astype(BF16), wd_ref[off:off + w, :])
        y = part if y is None else y + part
    return y


def _ple_epilogue(h3, p_ref, gple_ref, wpg_ref, wpp_ref, gfin_ref, final):
    gate = jax.nn.sigmoid(_dot(_rms(h3, gple_ref[...]).astype(BF16), wpg_ref[...]))
    h4 = h3 + _dot(p_ref[...].astype(BF16), wpp_ref[...]) * gate
    if final:
        h4 = _rms(h4, gfin_ref[...])
    return h4


def _ffn_kernel(hn_ref, h_ref, p_ref, wg_ref, wu_ref, wd_ref, gple_ref, wpg_ref, wpp_ref, gfin_ref, out_ref, acc,
                *, final):
    c = pl.program_id(1)

    @pl.when(c == 0)
    def _():
        acc[...] = jnp.zeros_like(acc)

    acc[...] += _swiglu_rows(hn_ref[...], wg_ref, wu_ref, wd_ref)

    @pl.when(c == FF_CHUNKS - 1)
    def _():
        out_ref[...] = _ple_epilogue(h_ref[...] + acc[...], p_ref, gple_ref, wpg_ref, wpp_ref, gfin_ref, final)


def _ffn(hn2, h2, p, l, wl, W, tm, final):
    T = h2.shape[0]
    jl = l // 2
    row = lambda i, c: (i, 0)
    const = lambda i, c: (0, 0)
    wg_spec = pl.BlockSpec((None, D_MODEL, FF_CHUNK), lambda i, c: (jl, 0, c))
    wd_spec = pl.BlockSpec((None, FF_CHUNK, D_MODEL), lambda i, c: (jl, c, 0))
    return pl.pallas_call(
        functools.partial(_ffn_kernel, final=final),
        grid=(T // tm, FF_CHUNKS),
        in_specs=[
            pl.BlockSpec((tm, D_MODEL), row),
            pl.BlockSpec((tm, D_MODEL), row),
            pl.BlockSpec((None, tm, D_PLE), lambda i, c: (l, i, 0)),
            wg_spec, wg_spec, wd_spec,
            pl.BlockSpec((1, D_MODEL), const),
            pl.BlockSpec((D_MODEL, D_MODEL), const),
            pl.BlockSpec((D_PLE, D_MODEL), const),
            pl.BlockSpec((1, D_MODEL), const),
        ],
        out_specs=pl.BlockSpec((tm, D_MODEL), row),
        out_shape=jax.ShapeDtypeStruct((T, D_MODEL), F32),
        scratch_shapes=[pltpu.VMEM((tm, D_MODEL), F32)],
        compiler_params=pltpu.CompilerParams(dimension_semantics=("arbitrary", "arbitrary"),
                                             vmem_limit_bytes=VMEM_LIMIT),
        name="ffn",
    )(hn2, h2, p, W["ff_wg"], W["ff_wu"], W["ff_wd"], wl["g_ple"], wl["ple_gate"], wl["ple_proj"],
      W["final_norm"])


ROW_TILE = 128
GATHER_SLAB = 256
EXPERT_TILE = 512
SEG_ALIGN = 16
GATE_LANES = HEAD_PAD
ROW_WIDTH = D_MODEL + GATE_LANES
TABLE_COLS = 3 * N_EXPERTS


def _moe_rows(tb):
    return 2 * tb + N_EXPERTS * ROW_TILE


def _scatter_slab(tb):
    return 1024 if _moe_rows(tb) % 1024 == 0 else 512


def _router_kernel(h_ref, g_ref, wr_ref, route_ref, meta_ref):
    tb = h_ref.shape[0]
    x32 = _rms(h_ref[...], g_ref[...])
    logits = jnp.dot(x32, wr_ref[...], preferred_element_type=F32, precision=lax.Precision.HIGHEST)
    lane_i = lax.broadcasted_iota(jnp.int32, logits.shape, 1)
    lane = lane_i.astype(F32)
    neg = jnp.float32(-jnp.inf)
    big = jnp.float32(HEAD_PAD)
    logits = jnp.where(lane_i < N_EXPERTS, logits, neg)
    v1 = jnp.max(logits, axis=-1, keepdims=True)
    i1 = jnp.min(jnp.where(logits == v1, lane, big), axis=-1, keepdims=True)
    rest = jnp.where(lane == i1, neg, logits)
    v2 = jnp.max(rest, axis=-1, keepdims=True)
    i2 = jnp.min(jnp.where(rest == v2, lane, big), axis=-1, keepdims=True)
    e2 = jnp.exp(v2 - v1)
    w1 = 1.0 / (1.0 + e2)
    w2 = e2 / (1.0 + e2)

    sel = jnp.where((lane == i1) | (lane == i2), 1.0, 0.0)
    ri = lax.broadcasted_iota(jnp.int32, (tb, tb), 0)
    ci = lax.broadcasted_iota(jnp.int32, (tb, tb), 1)
    earlier = jnp.where(ci < ri, 1.0, 0.0).astype(BF16)
    rank = _dot(earlier, sel.astype(BF16))
    counts = jnp.sum(sel, axis=0, keepdims=True)
    ntiles = jnp.floor((counts + (ROW_TILE - 1)) * (1.0 / ROW_TILE))
    nt8 = jnp.broadcast_to(ntiles, (SUBLANES, HEAD_PAD)).astype(BF16)
    a = lax.broadcasted_iota(jnp.int32, (HEAD_PAD, HEAD_PAD), 0)
    b = lax.broadcasted_iota(jnp.int32, (HEAD_PAD, HEAD_PAD), 1)
    start = _dot(nt8, jnp.where(a < b, 1.0, 0.0).astype(BF16))
    shifted = (b >= N_EXPERTS) & (b < 2 * N_EXPERTS) & (a < b - N_EXPERTS)
    start_hi = _dot(nt8, jnp.where(shifted, 1.0, 0.0).astype(BF16))
    base = start[0:1, :] * ROW_TILE + rank
    pos1 = jnp.sum(jnp.where(lane == i1, base, 0.0), axis=-1, keepdims=True)
    pos2 = jnp.sum(jnp.where(lane == i2, base, 0.0), axis=-1, keepdims=True)
    route_ref[...] = jnp.where(lane_i == 0, pos1, jnp.where(lane_i == 1, pos2,
                               jnp.where(lane_i == 2, w1, jnp.where(lane_i == 3, w2, 0.0))))
    sub = lax.broadcasted_iota(jnp.int32, (SUBLANES, HEAD_PAD), 0)
    tiles_row = jnp.broadcast_to(ntiles, (SUBLANES, HEAD_PAD)) + start_hi
    meta_ref[...] = jnp.where(sub == 1, jnp.broadcast_to(counts, (SUBLANES, HEAD_PAD)), tiles_row).astype(jnp.int32)


def _router(h2, wl, tb):
    T = h2.shape[0]
    nblk = T // tb
    return pl.pallas_call(
        _router_kernel,
        grid=(nblk,),
        in_specs=[
            pl.BlockSpec((tb, D_MODEL), lambda i: (i, 0)),
            pl.BlockSpec((1, D_MODEL), lambda i: (0, 0)),
            pl.BlockSpec((D_MODEL, HEAD_PAD), lambda i: (0, 0)),
        ],
        out_specs=[pl.BlockSpec((tb, HEAD_PAD), lambda i: (i, 0)),
                   pl.BlockSpec((SUBLANES, HEAD_PAD), lambda i: (i, 0))],
        out_shape=[jax.ShapeDtypeStruct((T, HEAD_PAD), F32),
                   jax.ShapeDtypeStruct((nblk * SUBLANES, HEAD_PAD), jnp.int32)],
        compiler_params=pltpu.CompilerParams(dimension_semantics=("arbitrary",),
                                             vmem_limit_bytes=VMEM_LIMIT),
        name="router",
    )(h2, wl["g_ffn"], wl["w_router"])


def _group_copies(tbl_ref, blk, local_ref, global_ref, sem, to_global):
    t0 = blk * TABLE_COLS

    def each(fn):
        for e in range(N_EXPERTS):
            ntile = tbl_ref[t0 + e]
            first = tbl_ref[t0 + N_EXPERTS + e]
            goff = tbl_ref[t0 + 2 * N_EXPERTS + e]

            def body(k, carry, first=first, goff=goff):
                loc = local_ref.at[pl.ds(pl.multiple_of((first + k) * ROW_TILE, ROW_TILE), ROW_TILE), :]
                glo = global_ref.at[pl.ds(pl.multiple_of(goff + k * ROW_TILE, SEG_ALIGN), ROW_TILE), :]
                fn(pltpu.make_async_copy(loc, glo, sem) if to_global else pltpu.make_async_copy(glo, loc, sem))
                return carry

            lax.fori_loop(0, ntile, body, 0)

    return each


def _used_rows(tbl_ref, blk):
    t0 = blk * TABLE_COLS
    last = N_EXPERTS - 1
    return (tbl_ref[t0 + N_EXPERTS + last] + tbl_ref[t0 + last]) * ROW_TILE


def _dispatch_kernel(tbl_ref, hn_ref, lanes_ref, xg_in_ref, xg_ref, xs, sem):
    del xg_in_ref
    blk = pl.program_id(0)
    tb = hn_ref.shape[0]
    rows = xs.shape[0]
    used = _used_rows(tbl_ref, blk)
    x = hn_ref[...]
    p1, p2 = lanes_ref[0:1, :], lanes_ref[1:2, :]
    w1, w2 = lanes_ref[2:3, :], lanes_ref[3:4, :]
    lane = lax.broadcasted_iota(jnp.int32, (GATHER_SLAB, GATE_LANES), 1)
    for r0 in range(0, rows, GATHER_SLAB):
        @pl.when(r0 < used)
        def _(r0=r0):
            rid = (lax.broadcasted_iota(jnp.int32, (GATHER_SLAB, tb), 0) + r0).astype(F32)
            hit1 = p1 == rid
            hit2 = p2 == rid
            onehot = jnp.where(hit1 | hit2, 1.0, 0.0).astype(BF16)
            xs[r0:r0 + GATHER_SLAB, :D_MODEL] = _dot(onehot, x).astype(BF16)
            wr = jnp.sum(jnp.where(hit1, w1, 0.0) + jnp.where(hit2, w2, 0.0), axis=-1, keepdims=True)
            hi = wr.astype(BF16).astype(F32)
            xs[r0:r0 + GATHER_SLAB, D_MODEL:] = jnp.where(lane == 0, hi, jnp.where(lane == 1, wr - hi, 0.0)
                                                           ).astype(BF16)

    copies = _group_copies(tbl_ref, blk, xs, xg_ref, sem, to_global=True)
    copies(lambda cp: cp.start())
    copies(lambda cp: cp.wait())


def _dispatch(hn2, lanes, table, zeros_rows, tb):
    T = hn2.shape[0]
    nblk = T // tb
    grid_spec = pltpu.PrefetchScalarGridSpec(
        num_scalar_prefetch=1,
        grid=(nblk,),
        in_specs=[
            pl.BlockSpec((tb, D_MODEL), lambda i, t: (i, 0)),
            pl.BlockSpec((None, SUBLANES, tb), lambda i, t: (i, 0, 0)),
            pl.BlockSpec(memory_space=pl.ANY),
        ],
        out_specs=pl.BlockSpec(memory_space=pl.ANY),
        scratch_shapes=[pltpu.VMEM((_moe_rows(tb), ROW_WIDTH), BF16), pltpu.SemaphoreType.DMA(())],
    )
    return pl.pallas_call(
        _dispatch_kernel,
        grid_spec=grid_spec,
        out_shape=jax.ShapeDtypeStruct(zeros_rows.shape, BF16),
        input_output_aliases={3: 0},
        compiler_params=pltpu.CompilerParams(dimension_semantics=("arbitrary",),
                                             vmem_limit_bytes=VMEM_LIMIT),
        name="dispatch",
    )(table, hn2, lanes, zeros_rows)


def _experts_kernel(te_ref, x_ref, wg_ref, wu_ref, wd_ref, y_ref, acc):
    t = pl.program_id(0)
    c = pl.program_id(1)
    ntile = pl.num_programs(0)
    live = t < te_ref[ntile]

    @pl.when(c == 0)
    def _():
        acc[...] = jnp.zeros_like(acc)

    @pl.when(live)
    def _():
        gate = x_ref[:, D_MODEL:D_MODEL + 1].astype(F32) + x_ref[:, D_MODEL + 1:D_MODEL + 2].astype(F32)
        acc[...] += _swiglu_rows(x_ref[:, :D_MODEL], wg_ref, wu_ref, wd_ref, row_scale=gate)

    @pl.when(c == FF_CHUNKS - 1)
    def _():
        y_ref[...] = acc[...].astype(BF16)


def _experts(xg, etable, l, W):
    rtot = xg.shape[0]
    jl = l // 2
    grid_spec = pltpu.PrefetchScalarGridSpec(
        num_scalar_prefetch=1,
        grid=(rtot // EXPERT_TILE, FF_CHUNKS),
        in_specs=[
            pl.BlockSpec((EXPERT_TILE, ROW_WIDTH), lambda t, c, te: (t, 0)),
            pl.BlockSpec((None, None, D_MODEL, FF_CHUNK), lambda t, c, te: (jl, te[t], 0, c)),
            pl.BlockSpec((None, None, D_MODEL, FF_CHUNK), lambda t, c, te: (jl, te[t], 0, c)),
            pl.BlockSpec((None, None, FF_CHUNK, D_MODEL), lambda t, c, te: (jl, te[t], c, 0)),
        ],
        out_specs=pl.BlockSpec((EXPERT_TILE, D_MODEL), lambda t, c, te: (t, 0)),
        scratch_shapes=[pltpu.VMEM((EXPERT_TILE, D_MODEL), F32)],
    )
    return pl.pallas_call(
        _experts_kernel,
        grid_spec=grid_spec,
        out_shape=jax.ShapeDtypeStruct((rtot, D_MODEL), BF16),
        compiler_params=pltpu.CompilerParams(dimension_semantics=("arbitrary", "arbitrary"),
                                             vmem_limit_bytes=VMEM_LIMIT),
        name="experts",
    )(etable, xg, W["moe_wg"], W["moe_wu"], W["moe_wd"])


def _collect_kernel(tbl_ref, route_ref, h_ref, p_ref, yg_ref, gple_ref, wpg_ref, wpp_ref, gfin_ref, out_ref,
                    ys, sem, *, final):
    blk = pl.program_id(0)
    tb = h_ref.shape[0]
    rows = ys.shape[0]
    slab = _scatter_slab(tb)
    used = _used_rows(tbl_ref, blk)
    copies = _group_copies(tbl_ref, blk, ys, yg_ref, sem, to_global=False)
    copies(lambda cp: cp.start())

    def clear(i, carry):
        ys[pl.ds(pl.multiple_of(i * ROW_TILE, ROW_TILE), ROW_TILE), :] = jnp.zeros((ROW_TILE, D_MODEL), BF16)
        return carry

    lax.fori_loop(used // ROW_TILE, rows // ROW_TILE, clear, 0)
    copies(lambda cp: cp.wait())

    p1c = route_ref[:, 0:1]
    p2c = route_ref[:, 1:2]
    out_ref[...] = h_ref[...]
    for r0 in range(0, rows, slab):
        @pl.when(r0 < used)
        def _(r0=r0):
            rid = (lax.broadcasted_iota(jnp.int32, (tb, slab), 1) + r0).astype(F32)
            onehot_t = jnp.where((p1c == rid) | (p2c == rid), 1.0, 0.0).astype(BF16)
            out_ref[...] += _dot(onehot_t, ys[r0:r0 + slab, :])
    out_ref[...] = _ple_epilogue(out_ref[...], p_ref, gple_ref, wpg_ref, wpp_ref, gfin_ref, final)


def _collect(route, h2, p, yg, table, l, wl, W, tb, final):
    T = h2.shape[0]
    row = lambda i, t: (i, 0)
    const = lambda i, t: (0, 0)
    grid_spec = pltpu.PrefetchScalarGridSpec(
        num_scalar_prefetch=1,
        grid=(T // tb,),
        in_specs=[
            pl.BlockSpec((tb, HEAD_PAD), row),
            pl.BlockSpec((tb, D_MODEL), row),
            pl.BlockSpec((None, tb, D_PLE), lambda i, t: (l, i, 0)),
            pl.BlockSpec(memory_space=pl.ANY),
            pl.BlockSpec((1, D_MODEL), const),
            pl.BlockSpec((D_MODEL, D_MODEL), const),
            pl.BlockSpec((D_PLE, D_MODEL), const),
            pl.BlockSpec((1, D_MODEL), const),
        ],
        out_specs=pl.BlockSpec((tb, D_MODEL), row),
        scratch_shapes=[pltpu.VMEM((_moe_rows(tb), D_MODEL), BF16), pltpu.SemaphoreType.DMA(())],
    )
    return pl.pallas_call(
        functools.partial(_collect_kernel, final=final),
        grid_spec=grid_spec,
        out_shape=jax.ShapeDtypeStruct((T, D_MODEL), F32),
        compiler_params=pltpu.CompilerParams(dimension_semantics=("arbitrary",),
                                             vmem_limit_bytes=VMEM_LIMIT),
        name="collect",
    )(table, route, h2, p, yg, wl["g_ple"], wl["ple_gate"], wl["ple_proj"], W["final_norm"])


def _global_rows(T, nblk):
    worst = 2 * T + (SEG_ALIGN - 1) * N_EXPERTS * nblk + N_EXPERTS * (ROW_TILE + EXPERT_TILE - 1)
    return -(-worst // EXPERT_TILE) * EXPERT_TILE


def _moe_plan(meta, T, nblk):
    m3 = meta.reshape(nblk, SUBLANES, HEAD_PAD)
    ntiles = m3[:, 0, :N_EXPERTS]
    first = m3[:, 0, N_EXPERTS:2 * N_EXPERTS]
    counts = m3[:, 1, :N_EXPERTS]
    seg = (counts + (SEG_ALIGN - 1)) // SEG_ALIGN * SEG_ALIGN
    region = (jnp.sum(seg, axis=0) + ROW_TILE + (EXPERT_TILE - 1)) // EXPERT_TILE * EXPERT_TILE
    region_end = jnp.cumsum(region)
    goff = (region_end - region)[None, :] + jnp.cumsum(seg, axis=0) - seg
    table = jnp.concatenate([ntiles, first, goff], axis=1).reshape(-1).astype(jnp.int32)
    ntile = _global_rows(T, nblk) // EXPERT_TILE
    tile_row = jnp.arange(ntile, dtype=jnp.int32) * EXPERT_TILE
    tile_expert = jnp.minimum(jnp.sum(tile_row[:, None] >= region_end[None, :], axis=1), N_EXPERTS - 1)
    etable = jnp.concatenate([tile_expert, region_end[-1:] // EXPERT_TILE]).astype(jnp.int32)
    return table, etable


def _moe_layer(hn2, h2, p, l, wl, W, tb, final):
    T = h2.shape[0]
    nblk = T // tb
    route, meta = _router(h2, wl, tb)
    table, etable = _moe_plan(meta, T, nblk)
    lanes = route[:, :SUBLANES].reshape(nblk, tb, SUBLANES).transpose(0, 2, 1)
    xg = _dispatch(hn2, lanes, table, jnp.zeros((_global_rows(T, nblk), ROW_WIDTH), BF16), tb)
    yg = _experts(xg, etable, l, W)
    return _collect(route, h2, p, yg, table, l, wl, W, tb, final)


def _rot_half(w):
    half = QK_ROPE // 2
    return jnp.concatenate([-w[..., half:], w[..., :half]], axis=-1)


def _prepare_weights(norm_mix, w_in, conv_w, conv_b, lru_wa, lru_ba, lru_wx, lru_bx, lru_lambda, q_norm, w_uq,
                     kv_norm, w_ukv, w_out, norm_ffn, ff_wg, ff_wu, ff_wd, moe_router, moe_wg, moe_wu, moe_wd,
                     norm_ple, ple_gate, ple_proj, final_norm):
    eye = jnp.eye(LRU_BLOCKS, dtype=F32)
    layers = []
    for l in range(DEPTH):
        w_kpe = w_in[l][:, D_IN_MAIN:]
        zl = jnp.zeros((D_MODEL, QK_NOPE), F32)
        zr = jnp.zeros((D_MODEL, HEAD_PAD - QK_NOPE - QK_ROPE), F32)
        w_in_ext = jnp.concatenate([w_in[l][:, :D_IN_MAIN], zl, w_kpe, zr, zl, _rot_half(w_kpe), zr], axis=-1)
        uq = w_uq[l]
        zq = jnp.zeros((Q_RANK, MLA_HEADS, HEAD_PAD - QK_NOPE - QK_ROPE), F32)
        wqa = jnp.concatenate([uq, zq], axis=-1).reshape(Q_RANK, MLA_PAD)
        wqb = jnp.concatenate([jnp.zeros((Q_RANK, MLA_HEADS, QK_NOPE), F32), _rot_half(uq[..., QK_NOPE:]), zq],
                              axis=-1).reshape(Q_RANK, MLA_PAD)
        ukv = w_ukv[l]
        zk = jnp.zeros((KV_RANK, MLA_HEADS, HEAD_PAD - QK_NOPE), F32)
        wk = jnp.concatenate([ukv[..., :QK_NOPE], zk], axis=-1).reshape(KV_RANK, MLA_PAD)
        wv = jnp.concatenate([ukv[..., QK_NOPE:], zk], axis=-1).reshape(KV_RANK, MLA_PAD)
        wo_mla = w_out[l][LRU_WIDTH:].reshape(MLA_HEADS, V_DIM, D_MODEL)
        wo_mla = jnp.pad(wo_mla, ((0, 0), (0, HEAD_PAD - V_DIM), (0, 0))).reshape(MLA_PAD, D_MODEL)

        def blockdiag(w):
            return jnp.einsum("xncd,nm->xncmd", w, eye).reshape(2, LRU_WIDTH, LRU_WIDTH)

        w_router = jnp.pad(moe_router[l // 2], ((0, 0), (0, HEAD_PAD - N_EXPERTS))) if l % 2 == 1 else None
        layers.append(dict(
            g_mix=norm_mix[l][None], w_in=w_in_ext.astype(BF16),
            q_norm=q_norm[l][None], wqa=wqa.astype(BF16), wqb=wqb.astype(BF16),
            kv_norm=kv_norm[l][None], wk=wk.astype(BF16), wv=wv.astype(BF16),
            conv_w=conv_w[l], conv_b=conv_b[l][None],
            wa=blockdiag(lru_wa[l]).astype(BF16), ba=lru_ba[l][:, None, :],
            wx=blockdiag(lru_wx[l]).astype(BF16), bx=lru_bx[l][:, None, :],
            lam=lru_lambda[l][:, None, :],
            w_out_lru=w_out[l][:LRU_WIDTH].astype(BF16), w_out_mla=wo_mla.astype(BF16),
            g_ffn=norm_ffn[l][None], w_router=w_router,
            g_ple=norm_ple[l][None], ple_gate=ple_gate[l].astype(BF16), ple_proj=ple_proj[l].astype(BF16),
        ))
    W = dict(ff_wg=ff_wg.astype(BF16), ff_wu=ff_wu.astype(BF16), ff_wd=ff_wd.astype(BF16),
             moe_wg=moe_wg.astype(BF16), moe_wu=moe_wu.astype(BF16), moe_wd=moe_wd.astype(BF16),
             final_norm=final_norm[None])
    return layers, W


def _rope_tables(S):
    pos = jnp.arange(S, dtype=F32)
    inv = ROPE_THETA ** (-jnp.arange(0, QK_ROPE, 2, dtype=F32) / QK_ROPE)
    ang = pos[:, None] * inv[None, :]
    cos, sin = jnp.cos(ang), jnp.sin(ang)
    pad = jnp.zeros((S, HEAD_PAD - QK_NOPE - QK_ROPE), F32)
    cos_t = jnp.concatenate([jnp.ones((S, QK_NOPE), F32), cos, cos, pad], axis=-1)
    sin_t = jnp.concatenate([jnp.zeros((S, QK_NOPE), F32), sin, sin, pad], axis=-1)
    return cos_t, sin_t


def _tiles(B, S):
    T = B * S
    return dict(tm=min(512, S), tc=min(1024, S), tq=min(1024, S), tf=min(512, T), tb=min(1024, T))


def _trunk(x, p, layers, W):
    B, S, _ = x.shape
    T = B * S
    t = _tiles(B, S)
    cos_t, sin_t = _rope_tables(S)
    h = x.reshape(T, D_MODEL)
    p = p.reshape(DEPTH, T, D_PLE)
    for l in range(DEPTH):
        wl = layers[l]
        final = (l == DEPTH - 1)
        xl, gg, q, kt, v = _inproj(h, wl, cos_t, sin_t, S, t["tm"])
        hf = _lru(xl, wl, 0, S, t["tc"], reverse=False)
        hb = _lru(xl, wl, 1, S, t["tc"], reverse=True)
        o = _attention(q, kt, v, B, S, t["tq"], t["tm"])
        h2, hn2 = _outproj(h, hf, hb, gg, o, wl, t["tm"])
        if l % 2 == 0:
            h = _ffn(hn2, h2, p, l, wl, W, t["tf"], final)
        else:
            h = _moe_layer(hn2, h2, p, l, wl, W, t["tb"], final)
    return h.reshape(B, S, D_MODEL)


def kernel(x_prompt, x_sample, p_prompt, p_sample, norm_mix, w_in, conv_w, conv_b, lru_wa, lru_ba, lru_wx, lru_bx, lru_lambda, q_norm, w_uq, kv_norm, w_ukv, w_out, norm_ffn, ff_wg, ff_wu, ff_wd, moe_router, moe_wg, moe_wu, moe_wd, norm_ple, ple_gate, ple_proj, final_norm):
    layers, W = _prepare_weights(norm_mix, w_in, conv_w, conv_b, lru_wa, lru_ba, lru_wx, lru_bx, lru_lambda,
                                 q_norm, w_uq, kv_norm, w_ukv, w_out, norm_ffn, ff_wg, ff_wu, ff_wd, moe_router,
                                 moe_wg, moe_wu, moe_wd, norm_ple, ple_gate, ple_proj, final_norm)
    y_prompt = _trunk(x_prompt, p_prompt, layers, W)
    y_sample = _trunk(x_sample, p_sample, layers, W)
    return (y_prompt, y_sample)
```
